```python
import jax, jax.numpy as jnp
from jax import lax
import numpy as np

D_MODEL = 1024
BATCH = 8
SEQ = 4096
DEPTH = 2
DEC_BATCH = 8
DEC_SEQ = 16
PAST_LEN = 4096

CHUNK = 64
N_EVEN = (DEPTH + 1) // 2
N_ODD = DEPTH // 2
RMS_EPS = 1e-6

CONV_DIM = D_MODEL // 2
CONV_WIDTH = 3
N_Q_HEADS = 8
N_KV_HEADS = 2
HEAD_DIM = 64
GQA_GROUP = N_Q_HEADS // N_KV_HEADS
ATTN_DIM = N_Q_HEADS * HEAD_DIM
KV_DIM = N_KV_HEADS * HEAD_DIM
WINDOW = 128
WINDOW_CHUNKS = WINDOW // CHUNK
BAND_KEYS = (WINDOW_CHUNKS + 1) * CHUNK
EVEN_SPLITS = [CONV_DIM, 2 * CONV_DIM, 3 * CONV_DIM, 3 * CONV_DIM + ATTN_DIM, 3 * CONV_DIM + ATTN_DIM + KV_DIM]
EVEN_IN_DIM = 3 * CONV_DIM + ATTN_DIM + 2 * KV_DIM
EVEN_MIX_DIM = CONV_DIM + ATTN_DIM
HGRN_EXPAND = 128
HGRN_HEADS = D_MODEL // HGRN_EXPAND
HGRN_DK = HGRN_EXPAND
HGRN_DV = D_MODEL // HGRN_HEADS
HGRN_BLOCK = 16
PEER_HEADS = 8
PEER_NKEYS = 128
PEER_N_EXPERTS = PEER_NKEYS * PEER_NKEYS
PEER_DKEY = 256
PEER_HALF = PEER_DKEY // 2
PEER_TOPK = 16
PEER_TOKEN_BLOCK = 128

kernel_name = 'hybrid_streaming_conv_swa_hgrn2_peer_step'


def rmsnorm(x, g):
    xf = x.astype(jnp.float32)
    y = xf * lax.rsqrt(jnp.mean(xf * xf, axis=-1, keepdims=True) + RMS_EPS)
    return (y * g.astype(jnp.float32)).astype(x.dtype)


def short_conv(u, buf, w):
    t = u.shape[1]
    up = jnp.concatenate([buf.astype(u.dtype), u], axis=1)
    y = w[0] * up[:, 0:t]
    for j in range(1, CONV_WIDTH):
        y = y + w[j] * up[:, j:j + t]
    return y, up[:, -(CONV_WIDTH - 1):]


def sink_attend(s, v, sinks, eq):
    sk = sinks.astype(jnp.float32).reshape(N_KV_HEADS, GQA_GROUP, 1, 1)
    m = jnp.maximum(jnp.max(s, axis=-1, keepdims=True), sk)
    p = jnp.exp(s - m)
    p = p / (jnp.sum(p, axis=-1, keepdims=True) + jnp.exp(sk - m))
    return jnp.einsum(eq, p.astype(v.dtype), v)


def swa_prompt(q, k, v, sinks):
    b, t = q.shape[:2]
    nc = t // CHUNK
    qc = q.reshape(b, nc, CHUNK, N_KV_HEADS, GQA_GROUP, HEAD_DIM)

    def band(z):
        zc = z.reshape(b, nc, CHUNK, N_KV_HEADS, HEAD_DIM)
        zp = jnp.pad(zc, ((0, 0), (WINDOW_CHUNKS, 0), (0, 0), (0, 0), (0, 0)))
        return jnp.concatenate([zp[:, j:j + nc] for j in range(WINDOW_CHUNKS + 1)], axis=2)

    kb, vb = band(k), band(v)
    key_chunk = jnp.arange(nc)[:, None] + (jnp.arange(BAND_KEYS) // CHUNK)[None, :] - WINDOW_CHUNKS
    valid = key_chunk >= 0
    s = jnp.einsum('bcqkgd,bcskd->bckgqs', qc, kb, preferred_element_type=jnp.float32) * (HEAD_DIM ** -0.5)
    s = jnp.where(valid[None, :, None, None, None, :], s, -jnp.inf)
    o = sink_attend(s, vb, sinks, 'bckgqs,bcskd->bcqkgd')
    return o.reshape(b, t, ATTN_DIM)


def swa_sample(q, k, v, k_cache, v_cache, sinks):
    b, t = q.shape[:2]
    kk = jnp.concatenate([k_cache.astype(k.dtype), k], axis=1)
    vv = jnp.concatenate([v_cache.astype(v.dtype), v], axis=1)
    qg = q.reshape(b, t, N_KV_HEADS, GQA_GROUP, HEAD_DIM)
    s = jnp.einsum('bqkgd,bskd->bkgqs', qg, kk, preferred_element_type=jnp.float32) * (HEAD_DIM ** -0.5)
    o = sink_attend(s, vv, sinks, 'bkgqs,bskd->bqkgd')
    return o.reshape(b, t, ATTN_DIM), kk[:, -WINDOW:], vv[:, -WINDOW:]


def even_mixer(xn, conv_buf, k_cache, v_cache, w_in, w_conv, q_gain, k_gain, sinks, w_out):
    b, t, _ = xn.shape
    z = xn @ w_in
    bg, cg, h, q, k, v = jnp.split(z, EVEN_SPLITS, axis=-1)
    conv_out, new_conv = short_conv(cg * h, conv_buf, w_conv)
    a_out = bg * conv_out
    q = rmsnorm(q.reshape(b, t, N_Q_HEADS, HEAD_DIM), q_gain)
    k = rmsnorm(k.reshape(b, t, N_KV_HEADS, HEAD_DIM), k_gain)
    v = v.reshape(b, t, N_KV_HEADS, HEAD_DIM)
    if k_cache is None:
        o = swa_prompt(q, k, v, sinks)
        new_k, new_v = k[:, -WINDOW:], v[:, -WINDOW:]
    else:
        o, new_k, new_v = swa_sample(q, k, v, k_cache, v_cache, sinks)
    mix = jnp.concatenate([a_out, o], axis=-1) @ w_out
    return mix, new_conv, new_k, new_v


def gla_blocks(q, k, v, logf, s0):
    b, t = q.shape[:2]
    L = HGRN_BLOCK
    n = -(-t // L)
    pad = n * L - t

    def blocks(z):
        zp = jnp.pad(z, ((0, 0), (0, pad), (0, 0), (0, 0)))
        return zp.reshape(b, n, L, z.shape[2], z.shape[3]).transpose(1, 0, 3, 2, 4)

    qb, kb, vb, fb = blocks(q), blocks(k), blocks(v), blocks(logf)
    causal = jnp.tril(jnp.ones((L, L), dtype=bool))

    def step(S, blk):
        qi, ki, vi, fi = blk
        cum = lax.cumsum(fi, axis=2)
        q_t = qi * jnp.exp(cum)
        k_t = ki * jnp.exp(-cum)
        A = jnp.where(causal, jnp.einsum('bhtd,bhsd->bhts', q_t, k_t), 0.0)
        o = jnp.einsum('bhts,bhse->bhte', A, vi) + jnp.einsum('bhtd,bhde->bhte', q_t, S)
        last = cum[:, :, -1:, :]
        S_new = jnp.exp(last[:, :, 0, :])[..., None] * S + jnp.einsum('bhsd,bhse->bhde', ki * jnp.exp(last - cum), vi)
        return S_new, o

    S, ob = lax.scan(step, s0, (qb, kb, vb, fb))
    o = ob.transpose(1, 0, 3, 2, 4).reshape(b, n * L, q.shape[2], v.shape[3])[:, :t]
    return o, S


def hgrn2_mixer(xn, state, w_in, lb_logits, layer, out_gain, w_out):
    b, t, _ = xn.shape
    z = xn @ w_in
    q, f, i, g = jnp.split(z, 4, axis=-1)
    lbs = jax.nn.softmax(lb_logits.astype(jnp.float32), axis=0)
    lbs = jnp.cumsum(lbs, axis=0) - lbs[0]
    lb = lbs[layer]
    fg = lb + (1.0 - lb) * jax.nn.sigmoid(f.astype(jnp.float32))
    logf = jnp.log(fg).reshape(b, t, HGRN_HEADS, HGRN_DK)
    kk = (1.0 - fg).reshape(b, t, HGRN_HEADS, HGRN_DK)
    qq = q.astype(jnp.float32).reshape(b, t, HGRN_HEADS, HGRN_DK)
    vv = i.astype(jnp.float32).reshape(b, t, HGRN_HEADS, HGRN_DV)
    o, S = gla_blocks(qq, kk, vv, logf, state.astype(jnp.float32))
    o = rmsnorm(o.reshape(b, t, D_MODEL).astype(xn.dtype), out_gain) * jax.nn.silu(g)
    return o @ w_out, S.astype(state.dtype)


def peer_ffn(xn, w_query, sub_keys, u_tab, v_tab):
    shape = xn.shape
    xf = xn.reshape(-1, D_MODEL)
    n = xf.shape[0]
    TB = PEER_TOKEN_BLOCK
    nb = -(-n // TB)
    xp = jnp.pad(xf, ((0, nb * TB - n), (0, 0))).reshape(nb, TB, D_MODEL)
    keys = sub_keys.astype(jnp.float32)

    def one_block(xb):
        q = (xb @ w_query).astype(jnp.float32).reshape(TB, PEER_HEADS, 2, PEER_HALF)
        s = jnp.einsum('thpd,hpkd->thpk', q, keys)
        sv, si = lax.top_k(s, PEER_TOPK)
        cand = sv[:, :, 0, :, None] + sv[:, :, 1, None, :]
        cand_idx = si[:, :, 0, :, None] * PEER_NKEYS + si[:, :, 1, None, :]
        cv, ci = lax.top_k(cand.reshape(TB, PEER_HEADS, PEER_TOPK * PEER_TOPK), PEER_TOPK)
        idx = jnp.take_along_axis(cand_idx.reshape(TB, PEER_HEADS, PEER_TOPK * PEER_TOPK), ci, axis=-1)
        gate = jax.nn.softmax(cv, axis=-1)
        u = jnp.take(u_tab, idx, axis=0)
        act = jax.nn.gelu(jnp.einsum('thkd,td->thk', u, xb).astype(jnp.float32), approximate=False)
        vr = jnp.take(v_tab, idx, axis=0)
        return jnp.einsum('thk,thkd->td', (gate * act).astype(xb.dtype), vr)

    y = lax.map(one_block, xp)
    return y.reshape(nb * TB, D_MODEL)[:n].reshape(shape)


def setup_inputs(seed: int = 0) -> dict:
    key = jax.random.key(seed)
    ks = jax.random.split(key, 22)
    f32 = jnp.float32

    def nrm(k, shape, scale):
        return scale * jax.random.normal(k, shape, f32)

    def gain(k, shape):
        return 1.0 + 0.01 * jax.random.normal(k, shape, f32)

    return {
        'x_prompt': nrm(ks[0], (BATCH, SEQ, D_MODEL), 1.0),
        'x_sample': nrm(ks[1], (DEC_BATCH, DEC_SEQ, D_MODEL), 1.0),
        'cache_conv': nrm(ks[2], (N_EVEN, DEC_BATCH, CONV_WIDTH - 1, CONV_DIM), 0.5),
        'cache_k': nrm(ks[3], (N_EVEN, DEC_BATCH, WINDOW, N_KV_HEADS, HEAD_DIM), 1.0),
        'cache_v': nrm(ks[4], (N_EVEN, DEC_BATCH, WINDOW, N_KV_HEADS, HEAD_DIM), 1.0),
        'state_hgrn': nrm(ks[5], (N_ODD, DEC_BATCH, HGRN_HEADS, HGRN_DK, HGRN_DV), 0.5),
        'norm_mix': gain(ks[6], (DEPTH, D_MODEL)),
        'norm_ffn': gain(ks[7], (DEPTH, D_MODEL)),
        'even_w_in': nrm(ks[8], (N_EVEN, D_MODEL, EVEN_IN_DIM), D_MODEL ** -0.5),
        'even_conv_w': nrm(ks[9], (N_EVEN, CONV_WIDTH, CONV_DIM), CONV_WIDTH ** -0.5),
        'even_q_gain': gain(ks[10], (N_EVEN, HEAD_DIM)),
        'even_k_gain': gain(ks[11], (N_EVEN, HEAD_DIM)),
        'even_sinks': nrm(ks[12], (N_EVEN, N_Q_HEADS), 0.5),
        'even_w_out': nrm(ks[13], (N_EVEN, EVEN_MIX_DIM, D_MODEL), EVEN_MIX_DIM ** -0.5),
        'hgrn_w_in': nrm(ks[14], (N_ODD, D_MODEL, 4 * D_MODEL), D_MODEL ** -0.5),
        'hgrn_lb': nrm(ks[15], (DEPTH, D_MODEL), 0.1),
        'hgrn_out_gain': gain(ks[16], (N_ODD, D_MODEL)),
        'hgrn_w_out': nrm(ks[17], (N_ODD, D_MODEL, D_MODEL), D_MODEL ** -0.5),
        'peer_w_query': nrm(ks[18], (DEPTH, D_MODEL, PEER_HEADS * PEER_DKEY), D_MODEL ** -0.5),
        'peer_sub_keys': nrm(ks[19], (DEPTH, PEER_HEADS, 2, PEER_NKEYS, PEER_HALF), PEER_HALF ** -0.5),
        'peer_u': nrm(ks[20], (DEPTH, PEER_N_EXPERTS, D_MODEL), D_MODEL ** -0.5),
        'peer_v': nrm(ks[21], (DEPTH, PEER_N_EXPERTS, D_MODEL), (PEER_HEADS * PEER_TOPK) ** -0.5),
    }


def reference(x_prompt, x_sample, cache_conv, cache_k, cache_v, state_hgrn, norm_mix, norm_ffn,
              even_w_in, even_conv_w, even_q_gain, even_k_gain, even_sinks, even_w_out,
              hgrn_w_in, hgrn_lb, hgrn_out_gain, hgrn_w_out,
              peer_w_query, peer_sub_keys, peer_u, peer_v):
    def run(x, conv_in, k_in, v_in, s_in):
        new_conv, new_k, new_v, new_s = [], [], [], []
        for l in range(DEPTH):
            j = l // 2
            xn = rmsnorm(x, norm_mix[l])
            if l % 2 == 0:
                mix, cb, kb, vb = even_mixer(
                    xn, conv_in[j], None if k_in is None else k_in[j], None if v_in is None else v_in[j],
                    even_w_in[j], even_conv_w[j], even_q_gain[j], even_k_gain[j], even_sinks[j], even_w_out[j])
                new_conv.append(cb)
                new_k.append(kb)
                new_v.append(vb)
            else:
                mix, sb = hgrn2_mixer(xn, s_in[j], hgrn_w_in[j], hgrn_lb, l, hgrn_out_gain[j], hgrn_w_out[j])
                new_s.append(sb)
            x = x + mix
            x = x + peer_ffn(rmsnorm(x, norm_ffn[l]), peer_w_query[l], peer_sub_keys[l], peer_u[l], peer_v[l])
        return x, jnp.stack(new_conv), jnp.stack(new_k), jnp.stack(new_v), jnp.stack(new_s)

    bp = x_prompt.shape[0]
    conv0 = jnp.zeros((N_EVEN, bp, CONV_WIDTH - 1, CONV_DIM), x_prompt.dtype)
    s0 = jnp.zeros((N_ODD, bp, HGRN_HEADS, HGRN_DK, HGRN_DV), x_prompt.dtype)
    y_prompt, conv_p, k_p, v_p, s_p = run(x_prompt, conv0, None, None, s0)
    y_sample, conv_s, k_s, v_s, s_s = run(x_sample, cache_conv, cache_k, cache_v, state_hgrn)
    return (y_prompt, y_sample, conv_p, k_p, v_p, s_p, conv_s, k_s, v_s, s_s)
```

```python
import functools

import jax
import jax.numpy as jnp
import numpy as np
from jax import lax
from jax.experimental import pallas as pl
from jax.experimental.pallas import tpu as pltpu

F32 = jnp.float32
BF16 = jnp.bfloat16

D_MODEL = 1024
RMS_EPS = 1e-6
CHUNK = 64
WINDOW = 128
CONV_DIM = 512
N_Q_HEADS = 8
N_KV_HEADS = 2
HEAD_DIM = 64
ATTN_DIM = 512
KV_DIM = 128
EVEN_IN_DIM = 2304
HGRN_HEADS = 8
HGRN_DK = 128
HGRN_BLOCK = 16
HGRN_CHUNK = 128
PEER_HEADS = 8
PEER_NKEYS = 128
PEER_TOPK = 16
PEER_N_EXPERTS = PEER_NKEYS * PEER_NKEYS

VMEM_LIMIT_BYTES = 52 * 1024 * 1024

NEG_INF = float("-inf")
POS_INF = float("inf")


def _rms_scale(x):
    return lax.rsqrt(jnp.mean(x * x, axis=-1, keepdims=True) + RMS_EPS)


def _split_dot(x, w_bf16):
    hi = x.astype(BF16)
    lo = (x - hi.astype(F32)).astype(BF16)
    return (jnp.dot(hi, w_bf16, preferred_element_type=F32)
            + jnp.dot(lo, w_bf16, preferred_element_type=F32))


def _even_kernel(x_ref, conv0_ref, kc0_ref, vc0_ref, g_ref, win_ref, convw_ref,
                 qg_ref, kg_ref, sink_ref, hsum_q_ref, hsum_k_ref, wout_ref,
                 y_ref, nconv_ref, nk_ref, nv_ref,
                 u_scr, k_scr, v_scr, mix_scr,
                 *, tile, valid_rows, has_cache):
    t = pl.program_id(1)

    @pl.when(t == 0)
    def _():
        u_scr[0:8, :] = jnp.zeros((8, CONV_DIM), F32)
        u_scr[6:8, :] = conv0_ref[0]
        k_scr[0:WINDOW, :] = kc0_ref[0]
        v_scr[0:WINDOW, :] = vc0_ref[0]

    x = x_ref[0]
    xn = x * _rms_scale(x) * g_ref[...]
    z = jnp.dot(xn.astype(BF16), win_ref[...], preferred_element_type=F32)
    bg = z[:, 0:512]
    cg = z[:, 512:1024]
    hh = z[:, 1024:1536]
    q = z[:, 1536:2048]
    k = z[:, 2048:2176]
    v = z[:, 2176:2304]

    u = cg * hh
    u_scr[8:8 + tile, :] = u
    cw = convw_ref[...]
    conv = (cw[0:1, :] * u_scr[6:6 + tile, :] + cw[1:2, :] * u_scr[7:7 + tile, :]
            + cw[2:3, :] * u)
    mix_scr[:, 0:CONV_DIM] = bg * conv
    tail = u_scr[6 + valid_rows:8 + valid_rows, :]
    nconv_ref[0] = tail
    u_scr[6:8, :] = tail

    q_ms = _split_dot(q * q, hsum_q_ref[...]) * (1.0 / HEAD_DIM)
    q = q * lax.rsqrt(q_ms + RMS_EPS) * qg_ref[...] * (HEAD_DIM ** -0.5)
    k_ms = _split_dot(k * k, hsum_k_ref[...]) * (1.0 / HEAD_DIM)
    k = k * lax.rsqrt(k_ms + RMS_EPS) * kg_ref[...]
    k_scr[WINDOW:WINDOW + tile, :] = k
    v_scr[WINDOW:WINDOW + tile, :] = v

    lane = lax.broadcasted_iota(jnp.int32, (CHUNK, KV_DIM), 1)
    low_half = lane < HEAD_DIM
    sink = sink_ref[...]
    nkeys = WINDOW + CHUNK
    col = lax.broadcasted_iota(jnp.int32, (N_Q_HEADS * CHUNK, nkeys), 1)
    for j in range(tile // CHUNK):
        r0 = j * CHUNK
        blocks = []
        for b in range(N_Q_HEADS):
            qv = q[r0:r0 + CHUNK, (b % 4) * KV_DIM:(b % 4 + 1) * KV_DIM]
            keep = low_half if b < 4 else jnp.logical_not(low_half)
            blocks.append(jnp.where(keep, qv, 0.0))
        qs = jnp.concatenate(blocks, axis=0).astype(BF16)
        kw = k_scr[r0:r0 + nkeys, :].astype(BF16)
        vw = v_scr[r0:r0 + nkeys, :].astype(BF16)
        s = lax.dot_general(qs, kw, (((1,), (1,)), ((), ())),
                            preferred_element_type=F32)
        if valid_rows < tile:
            s = jnp.where(col < WINDOW + valid_rows, s, NEG_INF)
        if not has_cache and r0 < WINDOW:
            s = jnp.where(jnp.logical_or(col >= WINDOW - r0, t > 0), s, NEG_INF)
        m = jnp.maximum(jnp.max(s, axis=-1, keepdims=True), sink)
        p = jnp.exp(s - m)
        p = p / (jnp.sum(p, axis=-1, keepdims=True) + jnp.exp(sink - m))
        o = jnp.dot(p.astype(BF16), vw, preferred_element_type=F32)
        for jj in range(4):
            oj = jnp.where(low_half, o[jj * CHUNK:(jj + 1) * CHUNK, :],
                           o[(4 + jj) * CHUNK:(5 + jj) * CHUNK, :])
            mix_scr[r0:r0 + CHUNK, CONV_DIM + jj * KV_DIM:CONV_DIM + (jj + 1) * KV_DIM] = oj

    nk = k_scr[valid_rows:valid_rows + WINDOW, :]
    nv = v_scr[valid_rows:valid_rows + WINDOW, :]
    nk_ref[0] = nk
    nv_ref[0] = nv
    k_scr[0:WINDOW, :] = nk
    v_scr[0:WINDOW, :] = nv

    mix = jnp.dot(mix_scr[...].astype(BF16), wout_ref[...], preferred_element_type=F32)
    y_ref[0] = x + mix


def _even_layer(x, conv0, kc0, vc0, gain, win, convw, qg, kg, sink_rows, hsum_q, hsum_k, wout,
                *, tile, valid_rows, has_cache):
    b, s, _ = x.shape
    nt = s // tile
    const2 = lambda i, j: (0, 0)
    per_b = lambda i, j: (i, 0, 0)
    kern = functools.partial(_even_kernel, tile=tile, valid_rows=valid_rows, has_cache=has_cache)
    return pl.pallas_call(
        kern,
        grid=(b, nt),
        in_specs=[
            pl.BlockSpec((1, tile, D_MODEL), lambda i, j: (i, j, 0)),
            pl.BlockSpec((1, 2, CONV_DIM), per_b),
            pl.BlockSpec((1, WINDOW, KV_DIM), per_b),
            pl.BlockSpec((1, WINDOW, KV_DIM), per_b),
            pl.BlockSpec((1, D_MODEL), const2),
            pl.BlockSpec((D_MODEL, EVEN_IN_DIM), const2),
            pl.BlockSpec((3, CONV_DIM), const2),
            pl.BlockSpec((1, ATTN_DIM), const2),
            pl.BlockSpec((1, KV_DIM), const2),
            pl.BlockSpec((N_Q_HEADS * CHUNK, 1), const2),
            pl.BlockSpec((ATTN_DIM, ATTN_DIM), const2),
            pl.BlockSpec((KV_DIM, KV_DIM), const2),
            pl.BlockSpec((D_MODEL, D_MODEL), const2),
        ],
        out_specs=[
            pl.BlockSpec((1, tile, D_MODEL), lambda i, j: (i, j, 0)),
            pl.BlockSpec((1, 2, CONV_DIM), per_b),
            pl.BlockSpec((1, WINDOW, KV_DIM), per_b),
            pl.BlockSpec((1, WINDOW, KV_DIM), per_b),
        ],
        out_shape=[
            jax.ShapeDtypeStruct((b, s, D_MODEL), F32),
            jax.ShapeDtypeStruct((b, 2, CONV_DIM), F32),
            jax.ShapeDtypeStruct((b, WINDOW, KV_DIM), F32),
            jax.ShapeDtypeStruct((b, WINDOW, KV_DIM), F32),
        ],
        scratch_shapes=[
            pltpu.VMEM((8 + tile, CONV_DIM), F32),
            pltpu.VMEM((WINDOW + tile, KV_DIM), F32),
            pltpu.VMEM((WINDOW + tile, KV_DIM), F32),
            pltpu.VMEM((tile, D_MODEL), F32),
        ],
        compiler_params=pltpu.CompilerParams(
            dimension_semantics=("arbitrary", "arbitrary"),
            vmem_limit_bytes=VMEM_LIMIT_BYTES),
        name="even_layer",
    )(x, conv0, kc0, vc0, gain, win, convw, qg, kg, sink_rows, hsum_q, hsum_k, wout)


def _hgrn_kernel(x_ref, s0_ref, g_ref, win_ref, lb_ref, og_ref, wout_ref,
                 y_ref, snew_ref,
                 s_scr, q_scr, k_scr, v_scr, lf_scr, o_scr,
                 *, tile, valid_rows):
    t = pl.program_id(1)

    @pl.when(t == 0)
    def _():
        s_scr[...] = s0_ref[0]

    x = x_ref[0]
    xn = x * _rms_scale(x) * g_ref[...]
    z = jnp.dot(xn.astype(BF16), win_ref[...], preferred_element_type=F32)
    lb = lb_ref[...]
    fg = lb + (1.0 - lb) * jax.nn.sigmoid(z[:, D_MODEL:2 * D_MODEL])
    logf = jnp.log(fg)
    kk = 1.0 - fg
    if valid_rows < tile:
        row = lax.broadcasted_iota(jnp.int32, (tile, D_MODEL), 0)
        live = row < valid_rows
        logf = jnp.where(live, logf, 0.0)
        kk = jnp.where(live, kk, 0.0)
    q_scr[...] = z[:, 0:D_MODEL]
    k_scr[...] = kk
    v_scr[...] = z[:, 2 * D_MODEL:3 * D_MODEL]
    gate = z[:, 3 * D_MODEL:4 * D_MODEL]

    ri = lax.broadcasted_iota(jnp.int32, (tile, tile), 0)
    ci = lax.broadcasted_iota(jnp.int32, (tile, tile), 1)
    same = (ri // HGRN_CHUNK) == (ci // HGRN_CHUNK)
    tril = jnp.where(jnp.logical_and(same, ci <= ri), 1.0, 0.0).astype(BF16)
    hi = logf.astype(BF16)
    lo = (logf - hi.astype(F32)).astype(BF16)
    lf_scr[...] = (jnp.dot(tril, hi, preferred_element_type=F32)
                   + jnp.dot(tril, lo, preferred_element_type=F32))

    for c in range(tile // HGRN_CHUNK):
        c0 = c * HGRN_CHUNK
        g = lf_scr[c0:c0 + HGRN_CHUNK, :]
        gtot = lf_scr[c0 + HGRN_CHUNK - 1:c0 + HGRN_CHUNK, :]
        qe = (q_scr[c0:c0 + HGRN_CHUNK, :] * jnp.exp(g)).astype(BF16)
        kh = k_scr[c0:c0 + HGRN_CHUNK, :] * jnp.exp(gtot - g)
        kh_t = kh.T.astype(BF16)
        dec_t = jnp.broadcast_to(jnp.exp(gtot), (HGRN_CHUNK, D_MODEL)).T
        vc = v_scr[c0:c0 + HGRN_CHUNK, :].astype(BF16)
        for h in range(HGRN_HEADS):
            hs = slice(h * HGRN_DK, (h + 1) * HGRN_DK)
            s_h = s_scr[h]
            o_scr[c0:c0 + HGRN_CHUNK, hs] = jnp.dot(
                qe[:, hs], s_h.astype(BF16), preferred_element_type=F32)
            s_scr[h] = dec_t[hs, :] * s_h + jnp.dot(
                kh_t[hs, :], vc[:, hs], preferred_element_type=F32)
        for j in range(HGRN_CHUNK // HGRN_BLOCK):
            r0 = c0 + j * HGRN_BLOCK
            nrow = HGRN_CHUNK - j * HGRN_BLOCK
            gj = lf_scr[r0:c0 + HGRN_CHUNK, :]
            if j == 0:
                rel = gj
            else:
                rel = gj - lf_scr[r0 - 1:r0, :]
            qj = (q_scr[r0:c0 + HGRN_CHUNK, :] * jnp.exp(rel)).astype(BF16)
            kj = (k_scr[r0:r0 + HGRN_BLOCK, :] * jnp.exp(-rel[0:HGRN_BLOCK, :])).astype(BF16)
            vj = v_scr[r0:r0 + HGRN_BLOCK, :].astype(BF16)
            causal = (lax.broadcasted_iota(jnp.int32, (nrow, HGRN_BLOCK), 0)
                      >= lax.broadcasted_iota(jnp.int32, (nrow, HGRN_BLOCK), 1))
            for h in range(HGRN_HEADS):
                hs = slice(h * HGRN_DK, (h + 1) * HGRN_DK)
                a = lax.dot_general(qj[:, hs], kj[:, hs], (((1,), (1,)), ((), ())),
                                    preferred_element_type=F32)
                a = jnp.where(causal, a, 0.0).astype(BF16)
                o_scr[r0:c0 + HGRN_CHUNK, hs] += jnp.dot(
                    a, vj[:, hs], preferred_element_type=F32)

    snew_ref[0] = s_scr[...]
    o = o_scr[...]
    o = o * _rms_scale(o) * og_ref[...]
    o = o * (gate * jax.nn.sigmoid(gate))
    y_ref[0] = x + jnp.dot(o.astype(BF16), wout_ref[...], preferred_element_type=F32)


def _hgrn_layer(x, s0, gain, win, lb, out_gain, wout, *, tile, valid_rows):
    b, s, _ = x.shape
    nt = s // tile
    const2 = lambda i, j: (0, 0)
    kern = functools.partial(_hgrn_kernel, tile=tile, valid_rows=valid_rows)
    return pl.pallas_call(
        kern,
        grid=(b, nt),
        in_specs=[
            pl.BlockSpec((1, tile, D_MODEL), lambda i, j: (i, j, 0)),
            pl.BlockSpec((1, HGRN_HEADS, HGRN_DK, HGRN_DK), lambda i, j: (i, 0, 0, 0)),
            pl.BlockSpec((1, D_MODEL), const2),
            pl.BlockSpec((D_MODEL, 4 * D_MODEL), const2),
            pl.BlockSpec((1, D_MODEL), const2),
            pl.BlockSpec((1, D_MODEL), const2),
            pl.BlockSpec((D_MODEL, D_MODEL), const2),
        ],
        out_specs=[
            pl.BlockSpec((1, tile, D_MODEL), lambda i, j: (i, j, 0)),
            pl.BlockSpec((1, HGRN_HEADS, HGRN_DK, HGRN_DK), lambda i, j: (i, 0, 0, 0)),
        ],
        out_shape=[
            jax.ShapeDtypeStruct((b, s, D_MODEL), F32),
            jax.ShapeDtypeStruct((b, HGRN_HEADS, HGRN_DK, HGRN_DK), F32),
        ],
        scratch_shapes=[
            pltpu.VMEM((HGRN_HEADS, HGRN_DK, HGRN_DK), F32),
            pltpu.VMEM((tile, D_MODEL), F32),
            pltpu.VMEM((tile, D_MODEL), F32),
            pltpu.VMEM((tile, D_MODEL), F32),
            pltpu.VMEM((tile, D_MODEL), F32),
            pltpu.VMEM((tile, D_MODEL), F32),
        ],
        compiler_params=pltpu.CompilerParams(
            dimension_semantics=("arbitrary", "arbitrary"),
            vmem_limit_bytes=VMEM_LIMIT_BYTES),
        name="hgrn_layer",
    )(x, s0, gain, win, lb, out_gain, wout)


def _topk_values(work_ref, vals_ref):
    def body(kk, carry):
        s = work_ref[...]
        m = jnp.max(s, axis=0)
        vals_ref[kk] = m
        work_ref[...] = jnp.where(s == m[None], NEG_INF, s)
        return carry
    lax.fori_loop(0, PEER_TOPK, body, 0)


def _route_kernel(x_ref, g_ref, wq_ref, kb_ref,
                  xnt_ref, s2h_ref, e2h_ref, th_ref, e1_ref,
                  s1_scr, s2_scr, work_scr, a_scr, b_scr, cand_scr, tmp_scr,
                  *, tile):
    x = x_ref[...]
    xn = x * _rms_scale(x) * g_ref[...]
    xnt = xn.T.astype(BF16)
    xnt_ref[...] = xnt
    qt = jnp.dot(wq_ref[...], xnt, preferred_element_type=F32).astype(BF16)
    half = PEER_HEADS * 128
    s1 = jnp.dot(kb_ref[0], qt[0:half, :], preferred_element_type=F32)
    s2 = jnp.dot(kb_ref[1], qt[half:2 * half, :], preferred_element_type=F32)
    s1 = s1.reshape(PEER_NKEYS, PEER_HEADS, tile)
    s2 = s2.reshape(PEER_NKEYS, PEER_HEADS, tile)
    s1_scr[...] = s1
    s2_scr[...] = s2
    work_scr[...] = s1
    _topk_values(work_scr, a_scr)
    work_scr[...] = s2
    _topk_values(work_scr, b_scr)

    pairs = [(i, j) for i in range(PEER_TOPK) for j in range(PEER_TOPK)
             if (i + 1) * (j + 1) <= PEER_TOPK]
    for n, (i, j) in enumerate(pairs):
        cand_scr[n] = a_scr[i] + b_scr[j]
    work2 = tmp_scr
    work2[...] = cand_scr[...]

    def tau_body(kk, tau):
        c = work2[...]
        m = jnp.max(c, axis=0)
        work2[...] = jnp.where(c == m[None], NEG_INF, c)
        return m
    tau = lax.fori_loop(0, PEER_TOPK, tau_body, jnp.zeros((PEER_HEADS, tile), F32))

    a0 = a_scr[0]
    b0 = b_scr[0]
    zsum = jnp.zeros((PEER_HEADS, tile), F32)
    thetas = []
    for i in range(PEER_TOPK):
        th_i = jnp.full((PEER_HEADS, tile), POS_INF, F32)
        ea = jnp.exp(a_scr[i] - a0)
        for n, (pi, pj) in enumerate(pairs):
            if pi != i:
                continue
            bj = b_scr[pj]
            sel = cand_scr[n] >= tau
            th_i = jnp.where(sel, jnp.minimum(th_i, bj), th_i)
            zsum = zsum + jnp.where(sel, ea * jnp.exp(bj - b0), 0.0)
        thetas.append(th_i)

    s1 = s1_scr[...]
    theta = jnp.full((PEER_NKEYS, PEER_HEADS, tile), POS_INF, F32)
    for i in range(PEER_TOPK):
        theta = jnp.where(s1 == a_scr[i][None], thetas[i][None], theta)
    th_ref[...] = theta.reshape(PEER_NKEYS * PEER_HEADS, tile)
    e1_ref[...] = jnp.exp(s1 - a0[None]).reshape(PEER_NKEYS * PEER_HEADS, tile)

    s2 = s2_scr[...]
    e2 = jnp.where(s2 >= b_scr[PEER_TOPK - 1][None],
                   jnp.exp(s2 - b0[None]) / zsum[None], 0.0)
    work_scr[...] = e2
    for h in range(PEER_HEADS):
        s2h_ref[h] = s2_scr[:, h, :]
        e2h_ref[h] = work_scr[:, h, :]


def _peer_route(x, gain, wq_t, kbig, *, tile):
    n = x.shape[0]
    nt = n // tile
    npairs = sum(1 for i in range(PEER_TOPK) for j in range(PEER_TOPK)
                 if (i + 1) * (j + 1) <= PEER_TOPK)
    rows = PEER_NKEYS * PEER_HEADS
    kern = functools.partial(_route_kernel, tile=tile)
    return pl.pallas_call(
        kern,
        grid=(nt,),
        in_specs=[
            pl.BlockSpec((tile, D_MODEL), lambda i: (i, 0)),
            pl.BlockSpec((1, D_MODEL), lambda i: (0, 0)),
            pl.BlockSpec((2 * rows, D_MODEL), lambda i: (0, 0)),
            pl.BlockSpec((2, rows, rows), lambda i: (0, 0, 0)),
        ],
        out_specs=[
            pl.BlockSpec((D_MODEL, tile), lambda i: (0, i)),
            pl.BlockSpec((PEER_HEADS, PEER_NKEYS, tile), lambda i: (0, 0, i)),
            pl.BlockSpec((PEER_HEADS, PEER_NKEYS, tile), lambda i: (0, 0, i)),
            pl.BlockSpec((rows, tile), lambda i: (0, i)),
            pl.BlockSpec((rows, tile), lambda i: (0, i)),
        ],
        out_shape=[
            jax.ShapeDtypeStruct((D_MODEL, n), BF16),
            jax.ShapeDtypeStruct((PEER_HEADS, PEER_NKEYS, n), F32),
            jax.ShapeDtypeStruct((PEER_HEADS, PEER_NKEYS, n), F32),
            jax.ShapeDtypeStruct((rows, n), F32),
            jax.ShapeDtypeStruct((rows, n), F32),
        ],
        scratch_shapes=[
            pltpu.VMEM((PEER_NKEYS, PEER_HEADS, tile), F32),
            pltpu.VMEM((PEER_NKEYS, PEER_HEADS, tile), F32),
            pltpu.VMEM((PEER_NKEYS, PEER_HEADS, tile), F32),
            pltpu.VMEM((PEER_TOPK, PEER_HEADS, tile), F32),
            pltpu.VMEM((PEER_TOPK, PEER_HEADS, tile), F32),
            pltpu.VMEM((npairs, PEER_HEADS, tile), F32),
            pltpu.VMEM((npairs, PEER_HEADS, tile), F32),
        ],
        compiler_params=pltpu.CompilerParams(
            dimension_semantics=("arbitrary",),
            vmem_limit_bytes=VMEM_LIMIT_BYTES),
        name="peer_route",
    )(x, gain, wq_t, kbig)


def _dense_kernel(x_ref, xnt_ref, s2h_ref, e2h_ref, th_ref, e1_ref, u_ref, vt_ref,
                  y_ref, acc_ref, *, tile, echunk):
    j = pl.program_id(1)

    @pl.when(j == 0)
    def _():
        acc_ref[...] = jnp.zeros_like(acc_ref)

    h_all = jnp.dot(u_ref[...], xnt_ref[...], preferred_element_type=F32)
    parts = []
    for a in range(echunk // PEER_NKEYS):
        c = j * (echunk // PEER_NKEYS) + a
        th8 = th_ref[pl.ds(pl.multiple_of(c * PEER_HEADS, PEER_HEADS), PEER_HEADS), :]
        e18 = e1_ref[pl.ds(pl.multiple_of(c * PEER_HEADS, PEER_HEADS), PEER_HEADS), :]
        gsum = jnp.zeros((PEER_NKEYS, tile), F32)
        for h in range(PEER_HEADS):
            hit = s2h_ref[h] >= th8[h:h + 1, :]
            gsum = gsum + jnp.where(hit, e2h_ref[h], 0.0) * e18[h:h + 1, :]
        hv = h_all[a * PEER_NKEYS:(a + 1) * PEER_NKEYS, :]
        act = 0.5 * hv * (1.0 + lax.erf(hv * (2.0 ** -0.5)))
        parts.append((jnp.where(gsum > 0.0, act, 0.0) * gsum).astype(BF16))
    w = jnp.concatenate(parts, axis=0)
    acc_ref[...] += jnp.dot(vt_ref[...], w, preferred_element_type=F32)

    @pl.when(j == pl.num_programs(1) - 1)
    def _():
        y_ref[...] = x_ref[...] + acc_ref[...].T


def _peer_dense(x, xnt, s2h, e2h, theta, e1, u_bf, vt_bf, *, tile, echunk):
    n = x.shape[0]
    nt = n // tile
    nchunk = PEER_N_EXPERTS // echunk
    rows = PEER_NKEYS * PEER_HEADS
    kern = functools.partial(_dense_kernel, tile=tile, echunk=echunk)
    return pl.pallas_call(
        kern,
        grid=(nt, nchunk),
        in_specs=[
            pl.BlockSpec((tile, D_MODEL), lambda i, j: (i, 0)),
            pl.BlockSpec((D_MODEL, tile), lambda i, j: (0, i)),
            pl.BlockSpec((PEER_HEADS, PEER_NKEYS, tile), lambda i, j: (0, 0, i)),
            pl.BlockSpec((PEER_HEADS, PEER_NKEYS, tile), lambda i, j: (0, 0, i)),
            pl.BlockSpec((rows, tile), lambda i, j: (0, i)),
            pl.BlockSpec((rows, tile), lambda i, j: (0, i)),
            pl.BlockSpec((echunk, D_MODEL), lambda i, j: (j, 0)),
            pl.BlockSpec((D_MODEL, echunk), lambda i, j: (0, j)),
        ],
        out_specs=pl.BlockSpec((tile, D_MODEL), lambda i, j: (i, 0)),
        out_shape=jax.ShapeDtypeStruct((n, D_MODEL), F32),
        scratch_shapes=[pltpu.VMEM((D_MODEL, tile), F32)],
        compiler_params=pltpu.CompilerParams(
            dimension_semantics=("arbitrary", "arbitrary"),
            vmem_limit_bytes=VMEM_LIMIT_BYTES),
        name="peer_dense",
    )(x, xnt, s2h, e2h, theta, e1, u_bf, vt_bf)


def _peer(x, gain, wq_t, kbig, u_bf, vt_bf, *, route_tile, dense_tile, echunk):
    xnt, s2h, e2h, theta, e1 = _peer_route(x, gain, wq_t, kbig, tile=route_tile)
    return _peer_dense(x, xnt, s2h, e2h, theta, e1, u_bf, vt_bf, tile=dense_tile, echunk=echunk)


def _even_params(norm_g, w_in, conv_w, q_gain, k_gain, sinks, w_out):
    qcols = np.array([1536 + (j + 4 * hf) * HEAD_DIM + d
                      for j in range(4) for hf in range(2) for d in range(HEAD_DIM)])
    cols = np.concatenate([np.arange(1536), qcols, np.arange(2048, EVEN_IN_DIM)])
    orow = np.array([CONV_DIM + (j + 4 * hf) * HEAD_DIM + d
                     for j in range(4) for hf in range(2) for d in range(HEAD_DIM)])
    rows = np.concatenate([np.arange(CONV_DIM), orow])
    blk = lambda n: jnp.asarray(
        (np.arange(n)[:, None] // HEAD_DIM) == (np.arange(n)[None, :] // HEAD_DIM)).astype(BF16)
    return dict(
        gain=norm_g.reshape(1, D_MODEL),
        win=w_in[:, cols].astype(BF16),
        convw=conv_w,
        qg=jnp.tile(q_gain, N_Q_HEADS).reshape(1, ATTN_DIM),
        kg=jnp.tile(k_gain, N_KV_HEADS).reshape(1, KV_DIM),
        sink_rows=jnp.repeat(sinks, CHUNK).reshape(N_Q_HEADS * CHUNK, 1),
        hsum_q=blk(ATTN_DIM),
        hsum_k=blk(KV_DIM),
        wout=w_out[rows, :].astype(BF16),
    )


def _peer_params(norm_g, w_query, sub_keys, u_tab, v_tab):
    wq_t = w_query.T.reshape(PEER_HEADS, 2, 128, D_MODEL).transpose(1, 0, 2, 3)
    wq_t = wq_t.reshape(2 * PEER_HEADS * 128, D_MODEL).astype(BF16)
    eye = jnp.eye(PEER_HEADS, dtype=sub_keys.dtype)
    kbig = jnp.einsum('hpkd,hg->pkhgd', sub_keys, eye).reshape(
        2, PEER_NKEYS * PEER_HEADS, PEER_HEADS * 128).astype(BF16)
    return dict(gain=norm_g.reshape(1, D_MODEL), wq_t=wq_t, kbig=kbig,
                u_bf=u_tab.astype(BF16), vt_bf=v_tab.T.astype(BF16))


def kernel(x_prompt, x_sample, cache_conv, cache_k, cache_v, state_hgrn, norm_mix, norm_ffn,
           even_w_in, even_conv_w, even_q_gain, even_k_gain, even_sinks, even_w_out,
           hgrn_w_in, hgrn_lb, hgrn_out_gain, hgrn_w_out,
           peer_w_query, peer_sub_keys, peer_u, peer_v):
    bp, sp, _ = x_prompt.shape
    bs, ss, _ = x_sample.shape

    ev = _even_params(norm_mix[0], even_w_in[0], even_conv_w[0], even_q_gain[0],
                      even_k_gain[0], even_sinks[0], even_w_out[0])
    pe = [_peer_params(norm_ffn[l], peer_w_query[l], peer_sub_keys[l], peer_u[l], peer_v[l])
          for l in range(2)]
    lbs = jax.nn.softmax(hgrn_lb.astype(F32), axis=0)
    lbs = jnp.cumsum(lbs, axis=0) - lbs[0]
    hg = dict(gain=norm_mix[1].reshape(1, D_MODEL), win=hgrn_w_in[0].astype(BF16),
              lb=lbs[1].reshape(1, D_MODEL), out_gain=hgrn_out_gain[0].reshape(1, D_MODEL),
              wout=hgrn_w_out[0].astype(BF16))

    def peer(x2d, l, route_tile, dense_tile):
        return _peer(x2d, pe[l]['gain'], pe[l]['wq_t'], pe[l]['kbig'], pe[l]['u_bf'],
                     pe[l]['vt_bf'], route_tile=route_tile, dense_tile=dense_tile, echunk=512)

    zc = jnp.zeros((bp, 2, CONV_DIM), F32)
    zkv = jnp.zeros((bp, WINDOW, KV_DIM), F32)
    x, conv_p, k_p, v_p = _even_layer(x_prompt, zc, zkv, zkv, **ev,
                                      tile=256, valid_rows=256, has_cache=False)
    x = peer(x.reshape(bp * sp, D_MODEL), 0, 256, 512).reshape(bp, sp, D_MODEL)
    s0 = jnp.zeros((bp, HGRN_HEADS, HGRN_DK, HGRN_DK), F32)
    x, s_p = _hgrn_layer(x, s0, **hg, tile=256, valid_rows=256)
    y_prompt = peer(x.reshape(bp * sp, D_MODEL), 1, 256, 512).reshape(bp, sp, D_MODEL)

    xs = jnp.pad(x_sample, ((0, 0), (0, CHUNK - ss), (0, 0)))
    xs, conv_s, k_s, v_s = _even_layer(
        xs, cache_conv[0], cache_k[0].reshape(bs, WINDOW, KV_DIM),
        cache_v[0].reshape(bs, WINDOW, KV_DIM), **ev, tile=CHUNK, valid_rows=ss, has_cache=True)
    xs = peer(xs[:, :ss].reshape(bs * ss, D_MODEL), 0, bs * ss, bs * ss).reshape(bs, ss, D_MODEL)
    xs = jnp.pad(xs, ((0, 0), (0, HGRN_CHUNK - ss), (0, 0)))
    xs, s_s = _hgrn_layer(xs, state_hgrn[0], **hg, tile=HGRN_CHUNK, valid_rows=ss)
    y_sample = peer(xs[:, :ss].reshape(bs * ss, D_MODEL), 1, bs * ss, bs * ss).reshape(bs, ss, D_MODEL)

    kv5 = lambda a, b: a.reshape(1, b, WINDOW, N_KV_HEADS, HEAD_DIM)
    return (y_prompt, y_sample, conv_p[None], kv5(k_p, bp), kv5(v_p, bp), s_p[None],
            conv_s[None], kv5(k_s, bs), kv5(v_s, bs), s_s[None])
```

```python
import functools

import jax
import jax.numpy as jnp
import numpy as np
from jax import lax
from jax.experimental import pallas as pl
from jax.experimental.pallas import tpu as pltpu

F32 = jnp.float32
BF16 = jnp.bfloat16

D_MODEL = 1024
RMS_EPS = 1e-6
CHUNK = 64
WINDOW = 128
CONV_DIM = 512
N_Q_HEADS = 8
N_KV_HEADS = 2
HEAD_DIM = 64
ATTN_DIM = 512
KV_DIM = 128
EVEN_IN_DIM = 2304
HGRN_HEADS = 8
HGRN_DK = 128
HGRN_BLOCK = 16
HGRN_CHUNK = 128
PEER_HEADS = 8
PEER_NKEYS = 128
PEER_TOPK = 16
PEER_N_EXPERTS = PEER_NKEYS * PEER_NKEYS
PEER_EXPERT_CHUNK = 2048
PEER_PACK_BLOCK = 1024
LANES = 128
BF16_ROWS = 16
MXU_DIM = 256

VMEM_LIMIT_BYTES = 52 * 1024 * 1024

NEG_INF = float("-inf")
POS_INF = float("inf")


def _rms_scale(x):
    return lax.rsqrt(jnp.mean(x * x, axis=-1, keepdims=True) + RMS_EPS)


def _split_dot(x, w_bf16):
    hi = x.astype(BF16)
    lo = (x - hi.astype(F32)).astype(BF16)
    return (jnp.dot(hi, w_bf16, preferred_element_type=F32)
            + jnp.dot(lo, w_bf16, preferred_element_type=F32))


def _even_kernel(x_ref, conv0_ref, kc0_ref, vc0_ref, g_ref, win_ref, convw_ref,
                 qg_ref, kg_ref, sink_ref, hsum_q_ref, hsum_k_ref, wout_ref,
                 y_ref, nconv_ref, nk_ref, nv_ref,
                 u_scr, k_scr, v_scr, mix_scr,
                 *, tile, valid_rows, has_cache):
    t = pl.program_id(1)

    @pl.when(t == 0)
    def _():
        u_scr[0:8, :] = jnp.zeros((8, CONV_DIM), F32)
        u_scr[6:8, :] = conv0_ref[0]
        k_scr[0:WINDOW, :] = kc0_ref[0]
        v_scr[0:WINDOW, :] = vc0_ref[0]

    x = x_ref[0]
    xn = x * _rms_scale(x) * g_ref[...]
    z = jnp.dot(xn.astype(BF16), win_ref[...], preferred_element_type=F32)
    bg = z[:, 0:512]
    cg = z[:, 512:1024]
    hh = z[:, 1024:1536]
    q = z[:, 1536:2048]
    k = z[:, 2048:2176]
    v = z[:, 2176:2304]

    u = cg * hh
    u_scr[8:8 + tile, :] = u
    cw = convw_ref[...]
    conv = (cw[0:1, :] * u_scr[6:6 + tile, :] + cw[1:2, :] * u_scr[7:7 + tile, :]
            + cw[2:3, :] * u)
    mix_scr[:, 0:CONV_DIM] = bg * conv
    tail = u_scr[6 + valid_rows:8 + valid_rows, :]
    nconv_ref[0] = tail
    u_scr[6:8, :] = tail

    q_ms = _split_dot(q * q, hsum_q_ref[...]) * (1.0 / HEAD_DIM)
    q = q * lax.rsqrt(q_ms + RMS_EPS) * qg_ref[...] * (HEAD_DIM ** -0.5)
    k_ms = _split_dot(k * k, hsum_k_ref[...]) * (1.0 / HEAD_DIM)
    k = k * lax.rsqrt(k_ms + RMS_EPS) * kg_ref[...]
    k_scr[WINDOW:WINDOW + tile, :] = k
    v_scr[WINDOW:WINDOW + tile, :] = v

    lane = lax.broadcasted_iota(jnp.int32, (CHUNK, KV_DIM), 1)
    low_half = lane < HEAD_DIM
    sink = sink_ref[...]
    nkeys = WINDOW + CHUNK
    col = lax.broadcasted_iota(jnp.int32, (N_Q_HEADS * CHUNK, nkeys), 1)
    for j in range(tile // CHUNK):
        r0 = j * CHUNK
        blocks = []
        for b in range(N_Q_HEADS):
            qv = q[r0:r0 + CHUNK, (b % 4) * KV_DIM:(b % 4 + 1) * KV_DIM]
            keep = low_half if b < 4 else jnp.logical_not(low_half)
            blocks.append(jnp.where(keep, qv, 0.0))
        qs = jnp.concatenate(blocks, axis=0).astype(BF16)
        kw = k_scr[r0:r0 + nkeys, :].astype(BF16)
        vw = v_scr[r0:r0 + nkeys, :].astype(BF16)
        s = lax.dot_general(qs, kw, (((1,), (1,)), ((), ())),
                            preferred_element_type=F32)
        if valid_rows < tile:
            s = jnp.where(col < WINDOW + valid_rows, s, NEG_INF)
        if not has_cache and r0 < WINDOW:
            s = jnp.where(jnp.logical_or(col >= WINDOW - r0, t > 0), s, NEG_INF)
        m = jnp.maximum(jnp.max(s, axis=-1, keepdims=True), sink)
        p = jnp.exp(s - m)
        p = p / (jnp.sum(p, axis=-1, keepdims=True) + jnp.exp(sink - m))
        o = jnp.dot(p.astype(BF16), vw, preferred_element_type=F32)
        for jj in range(4):
            oj = jnp.where(low_half, o[jj * CHUNK:(jj + 1) * CHUNK, :],
                           o[(4 + jj) * CHUNK:(5 + jj) * CHUNK, :])
            mix_scr[r0:r0 + CHUNK, CONV_DIM + jj * KV_DIM:CONV_DIM + (jj + 1) * KV_DIM] = oj

    nk = k_scr[valid_rows:valid_rows + WINDOW, :]
    nv = v_scr[valid_rows:valid_rows + WINDOW, :]
    nk_ref[0] = nk
    nv_ref[0] = nv
    k_scr[0:WINDOW, :] = nk
    v_scr[0:WINDOW, :] = nv

    mix = jnp.dot(mix_scr[...].astype(BF16), wout_ref[...], preferred_element_type=F32)
    y_ref[0] = x + mix


def _even_layer(x, conv0, kc0, vc0, gain, win, convw, qg, kg, sink_rows, hsum_q, hsum_k, wout,
                *, tile, valid_rows, has_cache):
    b, s, _ = x.shape
    nt = s // tile
    const2 = lambda i, j: (0, 0)
    per_b = lambda i, j: (i, 0, 0)
    kern = functools.partial(_even_kernel, tile=tile, valid_rows=valid_rows, has_cache=has_cache)
    return pl.pallas_call(
        kern,
        grid=(b, nt),
        in_specs=[
            pl.BlockSpec((1, tile, D_MODEL), lambda i, j: (i, j, 0)),
            pl.BlockSpec((1, 2, CONV_DIM), per_b),
            pl.BlockSpec((1, WINDOW, KV_DIM), per_b),
            pl.BlockSpec((1, WINDOW, KV_DIM), per_b),
            pl.BlockSpec((1, D_MODEL), const2),
            pl.BlockSpec((D_MODEL, EVEN_IN_DIM), const2),
            pl.BlockSpec((3, CONV_DIM), const2),
            pl.BlockSpec((1, ATTN_DIM), const2),
            pl.BlockSpec((1, KV_DIM), const2),
            pl.BlockSpec((N_Q_HEADS * CHUNK, 1), const2),
            pl.BlockSpec((ATTN_DIM, ATTN_DIM), const2),
            pl.BlockSpec((KV_DIM, KV_DIM), const2),
            pl.BlockSpec((D_MODEL, D_MODEL), const2),
        ],
        out_specs=[
            pl.BlockSpec((1, tile, D_MODEL), lambda i, j: (i, j, 0)),
            pl.BlockSpec((1, 2, CONV_DIM), per_b),
            pl.BlockSpec((1, WINDOW, KV_DIM), per_b),
            pl.BlockSpec((1, WINDOW, KV_DIM), per_b),
        ],
        out_shape=[
            jax.ShapeDtypeStruct((b, s, D_MODEL), F32),
            jax.ShapeDtypeStruct((b, 2, CONV_DIM), F32),
            jax.ShapeDtypeStruct((b, WINDOW, KV_DIM), F32),
            jax.ShapeDtypeStruct((b, WINDOW, KV_DIM), F32),
        ],
        scratch_shapes=[
            pltpu.VMEM((8 + tile, CONV_DIM), F32),
            pltpu.VMEM((WINDOW + tile, KV_DIM), F32),
            pltpu.VMEM((WINDOW + tile, KV_DIM), F32),
            pltpu.VMEM((tile, D_MODEL), F32),
        ],
        compiler_params=pltpu.CompilerParams(
            dimension_semantics=("arbitrary", "arbitrary"),
            vmem_limit_bytes=VMEM_LIMIT_BYTES),
        name="even_layer",
    )(x, conv0, kc0, vc0, gain, win, convw, qg, kg, sink_rows, hsum_q, hsum_k, wout)


def _hgrn_kernel(x_ref, s0_ref, g_ref, win_ref, lb_ref, og_ref, wout_ref,
                 y_ref, snew_ref,
                 s_scr, q_scr, k_scr, v_scr, lf_scr, o_scr,
                 *, tile, valid_rows):
    t = pl.program_id(1)

    @pl.when(t == 0)
    def _():
        s_scr[...] = s0_ref[0]

    x = x_ref[0]
    xn = x * _rms_scale(x) * g_ref[...]
    z = jnp.dot(xn.astype(BF16), win_ref[...], preferred_element_type=F32)
    lb = lb_ref[...]
    fg = lb + (1.0 - lb) * jax.nn.sigmoid(z[:, D_MODEL:2 * D_MODEL])
    logf = jnp.log(fg)
    kk = 1.0 - fg
    if valid_rows < tile:
        row = lax.broadcasted_iota(jnp.int32, (tile, D_MODEL), 0)
        live = row < valid_rows
        logf = jnp.where(live, logf, 0.0)
        kk = jnp.where(live, kk, 0.0)
    q_scr[...] = z[:, 0:D_MODEL]
    k_scr[...] = kk
    v_scr[...] = z[:, 2 * D_MODEL:3 * D_MODEL]
    gate = z[:, 3 * D_MODEL:4 * D_MODEL]

    ri = lax.broadcasted_iota(jnp.int32, (tile, tile), 0)
    ci = lax.broadcasted_iota(jnp.int32, (tile, tile), 1)
    same = (ri // HGRN_CHUNK) == (ci // HGRN_CHUNK)
    tril = jnp.where(jnp.logical_and(same, ci <= ri), 1.0, 0.0).astype(BF16)
    hi = logf.astype(BF16)
    lo = (logf - hi.astype(F32)).astype(BF16)
    lf_scr[...] = (jnp.dot(tril, hi, preferred_element_type=F32)
                   + jnp.dot(tril, lo, preferred_element_type=F32))

    for c in range(tile // HGRN_CHUNK):
        c0 = c * HGRN_CHUNK
        g = lf_scr[c0:c0 + HGRN_CHUNK, :]
        gtot = lf_scr[c0 + HGRN_CHUNK - 1:c0 + HGRN_CHUNK, :]
        qe = (q_scr[c0:c0 + HGRN_CHUNK, :] * jnp.exp(g)).astype(BF16)
        kh = k_scr[c0:c0 + HGRN_CHUNK, :] * jnp.exp(gtot - g)
        kh_t = kh.T.astype(BF16)
        dec_t = jnp.broadcast_to(jnp.exp(gtot), (HGRN_CHUNK, D_MODEL)).T
        vc = v_scr[c0:c0 + HGRN_CHUNK, :].astype(BF16)
        for h in range(HGRN_HEADS):
            hs = slice(h * HGRN_DK, (h + 1) * HGRN_DK)
            s_h = s_scr[h]
            o_scr[c0:c0 + HGRN_CHUNK, hs] = jnp.dot(
                qe[:, hs], s_h.astype(BF16), preferred_element_type=F32)
            s_scr[h] = dec_t[hs, :] * s_h + jnp.dot(
                kh_t[hs, :], vc[:, hs], preferred_element_type=F32)
        for j in range(HGRN_CHUNK // HGRN_BLOCK):
            r0 = c0 + j * HGRN_BLOCK
            nrow = HGRN_CHUNK - j * HGRN_BLOCK
            gj = lf_scr[r0:c0 + HGRN_CHUNK, :]
            if j == 0:
                rel = gj
            else:
                rel = gj - lf_scr[r0 - 1:r0, :]
            qj = (q_scr[r0:c0 + HGRN_CHUNK, :] * jnp.exp(rel)).astype(BF16)
            kj = (k_scr[r0:r0 + HGRN_BLOCK, :] * jnp.exp(-rel[0:HGRN_BLOCK, :])).astype(BF16)
            vj = v_scr[r0:r0 + HGRN_BLOCK, :].astype(BF16)
            causal = (lax.broadcasted_iota(jnp.int32, (nrow, HGRN_BLOCK), 0)
                      >= lax.broadcasted_iota(jnp.int32, (nrow, HGRN_BLOCK), 1))
            for h in range(HGRN_HEADS):
                hs = slice(h * HGRN_DK, (h + 1) * HGRN_DK)
                a = lax.dot_general(qj[:, hs], kj[:, hs], (((1,), (1,)), ((), ())),
                                    preferred_element_type=F32)
                a = jnp.where(causal, a, 0.0).astype(BF16)
                o_scr[r0:c0 + HGRN_CHUNK, hs] += jnp.dot(
                    a, vj[:, hs], preferred_element_type=F32)

    snew_ref[0] = s_scr[...]
    o = o_scr[...]
    o = o * _rms_scale(o) * og_ref[...]
    o = o * (gate * jax.nn.sigmoid(gate))
    y_ref[0] = x + jnp.dot(o.astype(BF16), wout_ref[...], preferred_element_type=F32)


def _hgrn_layer(x, s0, gain, win, lb, out_gain, wout, *, tile, valid_rows):
    b, s, _ = x.shape
    nt = s // tile
    const2 = lambda i, j: (0, 0)
    kern = functools.partial(_hgrn_kernel, tile=tile, valid_rows=valid_rows)
    return pl.pallas_call(
        kern,
        grid=(b, nt),
        in_specs=[
            pl.BlockSpec((1, tile, D_MODEL), lambda i, j: (i, j, 0)),
            pl.BlockSpec((1, HGRN_HEADS, HGRN_DK, HGRN_DK), lambda i, j: (i, 0, 0, 0)),
            pl.BlockSpec((1, D_MODEL), const2),
            pl.BlockSpec((D_MODEL, 4 * D_MODEL), const2),
            pl.BlockSpec((1, D_MODEL), const2),
            pl.BlockSpec((1, D_MODEL), const2),
            pl.BlockSpec((D_MODEL, D_MODEL), const2),
        ],
        out_specs=[
            pl.BlockSpec((1, tile, D_MODEL), lambda i, j: (i, j, 0)),
            pl.BlockSpec((1, HGRN_HEADS, HGRN_DK, HGRN_DK), lambda i, j: (i, 0, 0, 0)),
        ],
        out_shape=[
            jax.ShapeDtypeStruct((b, s, D_MODEL), F32),
            jax.ShapeDtypeStruct((b, HGRN_HEADS, HGRN_DK, HGRN_DK), F32),
        ],
        scratch_shapes=[
            pltpu.VMEM((HGRN_HEADS, HGRN_DK, HGRN_DK), F32),
            pltpu.VMEM((tile, D_MODEL), F32),
            pltpu.VMEM((tile, D_MODEL), F32),
            pltpu.VMEM((tile, D_MODEL), F32),
            pltpu.VMEM((tile, D_MODEL), F32),
            pltpu.VMEM((tile, D_MODEL), F32),
        ],
        compiler_params=pltpu.CompilerParams(
            dimension_semantics=("arbitrary", "arbitrary"),
            vmem_limit_bytes=VMEM_LIMIT_BYTES),
        name="hgrn_layer",
    )(x, s0, gain, win, lb, out_gain, wout)


def _sort_network(n):
    pairs = []

    def merge(lo, hi, r):
        step = r * 2
        if step < hi - lo:
            merge(lo, hi, step)
            merge(lo + r, hi, step)
            pairs.extend((i, i + r) for i in range(lo + r, hi - r, step))
        else:
            pairs.append((lo, lo + r))

    def sort(lo, hi):
        if hi - lo >= 1:
            mid = lo + (hi - lo) // 2
            sort(lo, mid)
            sort(mid + 1, hi)
            merge(lo, hi, 1)

    sort(0, n - 1)
    return pairs


_SORT16 = _sort_network(PEER_TOPK)


def _top16_sorted(load_slab):
    lists = []
    for g in range(PEER_NKEYS // PEER_TOPK):
        v = [load_slab(PEER_TOPK * g + i) for i in range(PEER_TOPK)]
        for i, j in _SORT16:
            v[i], v[j] = jnp.maximum(v[i], v[j]), jnp.minimum(v[i], v[j])
        lists.append(v)
    while len(lists) > 1:
        merged = []
        for a, b in zip(lists[0::2], lists[1::2]):
            c = [jnp.maximum(a[i], b[PEER_TOPK - 1 - i]) for i in range(PEER_TOPK)]
            for d in (8, 4, 2, 1):
                for i in range(PEER_TOPK):
                    if not i & d:
                        c[i], c[i + d] = jnp.maximum(c[i], c[i + d]), jnp.minimum(c[i], c[i + d])
            merged.append(c)
        lists = merged
    return lists[0]


def _bf16_pair_words(x):
    bits = lax.bitcast_convert_type(x.astype(BF16).astype(F32), jnp.int32)
    return bits | lax.shift_right_logical(bits, 16)


def _route_kernel(x_ref, g_ref, wq_ref, kb_ref,
                  xnt_ref, q2h_ref, e2h_ref, thr_ref, e1_ref,
                  s1_scr, s2_scr, a_scr, b_scr, cand_scr, tmp_scr,
                  *, tile):
    x = x_ref[...]
    xn = x * _rms_scale(x) * g_ref[...]
    xnt = xn.T.astype(BF16)
    xnt_ref[...] = pltpu.bitcast(xnt, jnp.int32)
    qt = jnp.dot(wq_ref[...], xnt, preferred_element_type=F32).astype(BF16)
    half = PEER_HEADS * 128
    s1 = jnp.dot(kb_ref[0], qt[0:half, :], preferred_element_type=F32)
    s2 = jnp.dot(kb_ref[1], qt[half:2 * half, :], preferred_element_type=F32)
    s1_scr[...] = s1.reshape(PEER_NKEYS, PEER_HEADS, tile)
    s2_scr[...] = s2.reshape(PEER_NKEYS, PEER_HEADS, tile)
    for lt in range(tile // LANES):
        ls = slice(lt * LANES, (lt + 1) * LANES)
        for src, dst in ((s1_scr, a_scr), (s2_scr, b_scr)):
            top = _top16_sorted(lambda kidx, src=src: src[kidx, :, ls])
            for i in range(PEER_TOPK):
                dst[i, :, ls] = top[i]

    pairs = [(i, j) for i in range(PEER_TOPK) for j in range(PEER_TOPK)
             if (i + 1) * (j + 1) <= PEER_TOPK]
    for n, (i, j) in enumerate(pairs):
        cand_scr[n] = a_scr[i] + b_scr[j]
    work2 = tmp_scr
    work2[...] = cand_scr[...]

    def tau_body(kk, tau):
        c = work2[...]
        m = jnp.max(c, axis=0)
        work2[...] = jnp.where(c == m[None], NEG_INF, c)
        return m
    tau = lax.fori_loop(0, PEER_TOPK, tau_body, jnp.zeros((PEER_HEADS, tile), F32))

    a0 = a_scr[0]
    b0 = b_scr[0]
    zsum = jnp.zeros((PEER_HEADS, tile), F32)
    codes = []
    for i in range(PEER_TOPK):
        lam = jnp.zeros((PEER_HEADS, tile), F32)
        ea = jnp.exp(a_scr[i] - a0)
        for n, (pi, pj) in enumerate(pairs):
            if pi != i:
                continue
            sel = cand_scr[n] >= tau
            lam = lam + jnp.where(sel, 1.0, 0.0)
            zsum = zsum + jnp.where(sel, ea * jnp.exp(b_scr[pj] - b0), 0.0)
        codes.append((PEER_TOPK + 1.0) - lam)

    s1 = s1_scr[...]
    thr = jnp.full((PEER_NKEYS, PEER_HEADS, tile), PEER_TOPK + 1.0, F32)
    for i in range(PEER_TOPK):
        thr = jnp.where(s1 == a_scr[i][None], codes[i][None], thr)
    thr_w = _bf16_pair_words(thr).reshape(PEER_NKEYS * PEER_HEADS, tile)
    e1_w = _bf16_pair_words(jnp.exp(s1 - a0[None])).reshape(PEER_NKEYS * PEER_HEADS, tile)
    for lt in range(tile // LANES):
        thr_ref[lt] = thr_w[:, lt * LANES:(lt + 1) * LANES]
        e1_ref[lt] = e1_w[:, lt * LANES:(lt + 1) * LANES]

    s2hm = jnp.dot(kb_ref[2], qt[half:2 * half, :], preferred_element_type=F32)
    zscale = 0.5 / zsum
    for h in range(PEER_HEADS):
        s2h = s2hm[h * PEER_NKEYS:(h + 1) * PEER_NKEYS, :]
        q2 = jnp.zeros((PEER_NKEYS, tile), F32)
        for jj in range(PEER_TOPK - 1, -1, -1):
            q2 = jnp.where(s2h >= b_scr[jj, h:h + 1, :], float(PEER_TOPK - jj), q2)
        q2h_ref[h] = pltpu.bitcast(q2.astype(BF16), jnp.int32)
        e2 = jnp.where(q2 > 0.0, jnp.exp(s2h - b0[h:h + 1, :]) * zscale[h:h + 1, :], 0.0)
        e2h_ref[h] = pltpu.bitcast(e2.astype(BF16), jnp.int32)


def _peer_route(x, gain, wq_t, kbig, *, tile):
    n = x.shape[0]
    nt = n // tile
    npairs = sum(1 for i in range(PEER_TOPK) for j in range(PEER_TOPK)
                 if (i + 1) * (j + 1) <= PEER_TOPK)
    rows = PEER_NKEYS * PEER_HEADS
    kern = functools.partial(_route_kernel, tile=tile)
    return pl.pallas_call(
        kern,
        grid=(nt,),
        in_specs=[
            pl.BlockSpec((tile, D_MODEL), lambda i: (i, 0)),
            pl.BlockSpec((1, D_MODEL), lambda i: (0, 0)),
            pl.BlockSpec((2 * rows, D_MODEL), lambda i: (0, 0)),
            pl.BlockSpec((3, rows, rows), lambda i: (0, 0, 0)),
        ],
        out_specs=[
            pl.BlockSpec((D_MODEL // 2, tile), lambda i: (0, i)),
            pl.BlockSpec((PEER_HEADS, PEER_NKEYS // 2, tile), lambda i: (0, 0, i)),
            pl.BlockSpec((PEER_HEADS, PEER_NKEYS // 2, tile), lambda i: (0, 0, i)),
            pl.BlockSpec((tile // LANES, rows, LANES), lambda i: (i, 0, 0)),
            pl.BlockSpec((tile // LANES, rows, LANES), lambda i: (i, 0, 0)),
        ],
        out_shape=[
            jax.ShapeDtypeStruct((D_MODEL // 2, n), jnp.int32),
            jax.ShapeDtypeStruct((PEER_HEADS, PEER_NKEYS // 2, n), jnp.int32),
            jax.ShapeDtypeStruct((PEER_HEADS, PEER_NKEYS // 2, n), jnp.int32),
            jax.ShapeDtypeStruct((n // LANES, rows, LANES), jnp.int32),
            jax.ShapeDtypeStruct((n // LANES, rows, LANES), jnp.int32),
        ],
        scratch_shapes=[
            pltpu.VMEM((PEER_NKEYS, PEER_HEADS, tile), F32),
            pltpu.VMEM((PEER_NKEYS, PEER_HEADS, tile), F32),
            pltpu.VMEM((PEER_TOPK, PEER_HEADS, tile), F32),
            pltpu.VMEM((PEER_TOPK, PEER_HEADS, tile), F32),
            pltpu.VMEM((npairs, PEER_HEADS, tile), F32),
            pltpu.VMEM((npairs, PEER_HEADS, tile), F32),
        ],
        compiler_params=pltpu.CompilerParams(
            dimension_semantics=("arbitrary",),
            vmem_limit_bytes=VMEM_LIMIT_BYTES),
        name="peer_route",
    )(x, gain, wq_t, kbig)


def _dense_kernel(x_ref, xnt_ref, q2h_ref, e2h_ref, thr_ref, e1_ref, u_ref, vt_ref,
                  y_ref, acc_ref, w_scr, *, tile, echunk):
    j = pl.program_id(1)

    @pl.when(j == 0)
    def _():
        acc_ref[...] = jnp.zeros_like(acc_ref)

    mxu_w = min(tile, MXU_DIM)
    ngrp = PEER_NKEYS // BF16_ROWS

    def bcast_row(word_ref, r, l0):
        row = word_ref[l0 // LANES, r:r + 1, :]
        return pltpu.bitcast(jnp.broadcast_to(row, (8, LANES)), BF16)

    h_all = jnp.dot(pltpu.bitcast(u_ref[0], BF16), pltpu.bitcast(xnt_ref[...], BF16),
                         preferred_element_type=F32)
    na = echunk // PEER_NKEYS
    ablk = 4
    gblk = 4
    for mb in range(tile // mxu_w):
        m0 = mb * mxu_w
        for lb in range(mxu_w // LANES):
            l0 = m0 + lb * LANES
            for gb in range(ngrp // gblk):
                grp = [gb * gblk + g for g in range(gblk)]
                for ab in range(na // ablk):
                    keys = [ab * ablk + a for a in range(ablk)]
                    gsum = [[jnp.zeros((BF16_ROWS, LANES), BF16) for _ in grp] for _ in keys]
                    for h in range(PEER_HEADS):
                        q2 = [pltpu.bitcast(q2h_ref[h, 8 * g:8 * g + 8, l0:l0 + LANES], BF16)
                              for g in grp]
                        e2 = [pltpu.bitcast(e2h_ref[h, 8 * g:8 * g + 8, l0:l0 + LANES], BF16)
                              for g in grp]
                        for ai, a in enumerate(keys):
                            thr = bcast_row(thr_ref, a * PEER_HEADS + h, l0)
                            e1 = bcast_row(e1_ref, a * PEER_HEADS + h, l0)
                            for g in range(gblk):
                                hit = q2[g] >= thr
                                gsum[ai][g] = gsum[ai][g] + jnp.where(hit, e2[g], 0.0) * e1
                    for ai, a in enumerate(keys):
                        for g in range(gblk):
                            e0 = a * PEER_NKEYS + grp[g] * BF16_ROWS
                            hv = h_all[e0:e0 + BF16_ROWS, l0:l0 + LANES]
                            act = (hv * (1.0 + lax.erf(hv * (2.0 ** -0.5)))).astype(BF16)
                            w_scr[e0:e0 + BF16_ROWS, l0:l0 + LANES] = (
                                jnp.where(gsum[ai][g] > 0.0, act, 0.0) * gsum[ai][g])
        acc_ref[:, m0:m0 + mxu_w] += jnp.dot(
            pltpu.bitcast(vt_ref[0], BF16), w_scr[:, m0:m0 + mxu_w],
            preferred_element_type=F32)

    @pl.when(j == pl.num_programs(1) - 1)
    def _():
        y_ref[...] = x_ref[...] + acc_ref[...].T


def _peer_dense(x, xnt, q2h, e2h, thr, e1, u_words, vt_words, *, layer, tile, echunk):
    n = x.shape[0]
    nt = n // tile
    nchunk = PEER_N_EXPERTS // echunk
    crows = (echunk // PEER_NKEYS) * PEER_HEADS
    kern = functools.partial(_dense_kernel, tile=tile, echunk=echunk)
    return pl.pallas_call(
        kern,
        grid=(nt, nchunk),
        in_specs=[
            pl.BlockSpec((tile, D_MODEL), lambda i, j: (i, 0)),
            pl.BlockSpec((D_MODEL // 2, tile), lambda i, j: (0, i)),
            pl.BlockSpec((PEER_HEADS, PEER_NKEYS // 2, tile), lambda i, j: (0, 0, i)),
            pl.BlockSpec((PEER_HEADS, PEER_NKEYS // 2, tile), lambda i, j: (0, 0, i)),
            pl.BlockSpec((tile // LANES, crows, LANES), lambda i, j: (i, j, 0)),
            pl.BlockSpec((tile // LANES, crows, LANES), lambda i, j: (i, j, 0)),
            pl.BlockSpec((1, echunk // 2, D_MODEL), lambda i, j: (layer, j, 0)),
            pl.BlockSpec((1, D_MODEL // 2, echunk), lambda i, j: (layer, 0, j)),
        ],
        out_specs=pl.BlockSpec((tile, D_MODEL), lambda i, j: (i, 0)),
        out_shape=jax.ShapeDtypeStruct((n, D_MODEL), F32),
        scratch_shapes=[pltpu.VMEM((D_MODEL, tile), F32),
                        pltpu.VMEM((echunk, tile), BF16)],
        compiler_params=pltpu.CompilerParams(
            dimension_semantics=("arbitrary", "arbitrary"),
            vmem_limit_bytes=VMEM_LIMIT_BYTES),
        name="peer_dense",
    )(x, xnt, q2h, e2h, thr, e1, u_words, vt_words)


def _peer(x, gain, wq_t, kbig, u_words, vt_words, *, layer, route_tile, dense_tile, echunk):
    xnt, q2h, e2h, thr, e1 = _peer_route(x, gain, wq_t, kbig, tile=route_tile)
    return _peer_dense(x, xnt, q2h, e2h, thr, e1, u_words, vt_words, layer=layer,
                       tile=dense_tile, echunk=echunk)


def _even_params(norm_g, w_in, conv_w, q_gain, k_gain, sinks, w_out):
    qcols = np.array([1536 + (j + 4 * hf) * HEAD_DIM + d
                      for j in range(4) for hf in range(2) for d in range(HEAD_DIM)])
    cols = np.concatenate([np.arange(1536), qcols, np.arange(2048, EVEN_IN_DIM)])
    orow = np.array([CONV_DIM + (j + 4 * hf) * HEAD_DIM + d
                     for j in range(4) for hf in range(2) for d in range(HEAD_DIM)])
    rows = np.concatenate([np.arange(CONV_DIM), orow])
    blk = lambda n: jnp.asarray(
        (np.arange(n)[:, None] // HEAD_DIM) == (np.arange(n)[None, :] // HEAD_DIM)).astype(BF16)
    return dict(
        gain=norm_g.reshape(1, D_MODEL),
        win=w_in[:, cols].astype(BF16),
        convw=conv_w,
        qg=jnp.tile(q_gain, N_Q_HEADS).reshape(1, ATTN_DIM),
        kg=jnp.tile(k_gain, N_KV_HEADS).reshape(1, KV_DIM),
        sink_rows=jnp.repeat(sinks, CHUNK).reshape(N_Q_HEADS * CHUNK, 1),
        hsum_q=blk(ATTN_DIM),
        hsum_k=blk(KV_DIM),
        wout=w_out[rows, :].astype(BF16),
    )


def _pack_tables_kernel(u_ref, v_ref, uo_ref, vo_ref):
    uo_ref[0] = pltpu.bitcast(u_ref[0].astype(BF16), jnp.int32)
    vo_ref[0] = pltpu.bitcast(v_ref[0].T.astype(BF16), jnp.int32)


def _pack_tables(u_tab, v_tab, *, eblk):
    nl, ne, d = u_tab.shape
    return pl.pallas_call(
        _pack_tables_kernel,
        grid=(nl, ne // eblk),
        in_specs=[pl.BlockSpec((1, eblk, d), lambda l, i: (l, i, 0)),
                  pl.BlockSpec((1, eblk, d), lambda l, i: (l, i, 0))],
        out_specs=[pl.BlockSpec((1, eblk // 2, d), lambda l, i: (l, i, 0)),
                   pl.BlockSpec((1, d // 2, eblk), lambda l, i: (l, 0, i))],
        out_shape=[jax.ShapeDtypeStruct((nl, ne // 2, d), jnp.int32),
                   jax.ShapeDtypeStruct((nl, d // 2, ne), jnp.int32)],
        compiler_params=pltpu.CompilerParams(
            dimension_semantics=("arbitrary", "arbitrary"),
            vmem_limit_bytes=VMEM_LIMIT_BYTES),
        name="pack_tables",
    )(u_tab, v_tab)


def _peer_params(norm_g, w_query, sub_keys):
    wq_t = w_query.T.reshape(PEER_HEADS, 2, 128, D_MODEL).transpose(1, 0, 2, 3)
    wq_t = wq_t.reshape(2 * PEER_HEADS * 128, D_MODEL).astype(BF16)
    eye = jnp.eye(PEER_HEADS, dtype=sub_keys.dtype)
    kbig = jnp.einsum('hpkd,hg->pkhgd', sub_keys, eye).reshape(
        2, PEER_NKEYS * PEER_HEADS, PEER_HEADS * 128)
    khm = jnp.einsum('hkd,hg->hkgd', sub_keys[:, 1], eye).reshape(
        1, PEER_HEADS * PEER_NKEYS, PEER_HEADS * 128)
    kbig = jnp.concatenate([kbig, khm], axis=0).astype(BF16)
    return dict(gain=norm_g.reshape(1, D_MODEL), wq_t=wq_t, kbig=kbig)


def kernel(x_prompt, x_sample, cache_conv, cache_k, cache_v, state_hgrn, norm_mix, norm_ffn,
           even_w_in, even_conv_w, even_q_gain, even_k_gain, even_sinks, even_w_out,
           hgrn_w_in, hgrn_lb, hgrn_out_gain, hgrn_w_out,
           peer_w_query, peer_sub_keys, peer_u, peer_v):
    bp, sp, _ = x_prompt.shape
    bs, ss, _ = x_sample.shape

    ev = _even_params(norm_mix[0], even_w_in[0], even_conv_w[0], even_q_gain[0],
                      even_k_gain[0], even_sinks[0], even_w_out[0])
    u_words, vt_words = _pack_tables(peer_u, peer_v, eblk=PEER_PACK_BLOCK)
    pe = [_peer_params(norm_ffn[l], peer_w_query[l], peer_sub_keys[l]) for l in range(2)]
    lbs = jax.nn.softmax(hgrn_lb.astype(F32), axis=0)
    lbs = jnp.cumsum(lbs, axis=0) - lbs[0]
    hg = dict(gain=norm_mix[1].reshape(1, D_MODEL), win=hgrn_w_in[0].astype(BF16),
              lb=lbs[1].reshape(1, D_MODEL), out_gain=hgrn_out_gain[0].reshape(1, D_MODEL),
              wout=hgrn_w_out[0].astype(BF16))

    def peer(x2d, l, route_tile, dense_tile):
        return _peer(x2d, pe[l]['gain'], pe[l]['wq_t'], pe[l]['kbig'], u_words, vt_words,
                     layer=l, route_tile=route_tile, dense_tile=dense_tile,
                     echunk=PEER_EXPERT_CHUNK)

    zc = jnp.zeros((bp, 2, CONV_DIM), F32)
    zkv = jnp.zeros((bp, WINDOW, KV_DIM), F32)
    x, conv_p, k_p, v_p = _even_layer(x_prompt, zc, zkv, zkv, **ev,
                                      tile=256, valid_rows=256, has_cache=False)
    x = peer(x.reshape(bp * sp, D_MODEL), 0, 256, 512).reshape(bp, sp, D_MODEL)
    s0 = jnp.zeros((bp, HGRN_HEADS, HGRN_DK, HGRN_DK), F32)
    x, s_p = _hgrn_layer(x, s0, **hg, tile=256, valid_rows=256)
    y_prompt = peer(x.reshape(bp * sp, D_MODEL), 1, 256, 512).reshape(bp, sp, D_MODEL)

    xs = jnp.pad(x_sample, ((0, 0), (0, CHUNK - ss), (0, 0)))
    xs, conv_s, k_s, v_s = _even_layer(
        xs, cache_conv[0], cache_k[0].reshape(bs, WINDOW, KV_DIM),
        cache_v[0].reshape(bs, WINDOW, KV_DIM), **ev, tile=CHUNK, valid_rows=ss, has_cache=True)
    xs = peer(xs[:, :ss].reshape(bs * ss, D_MODEL), 0, bs * ss, bs * ss).reshape(bs, ss, D_MODEL)
    xs = jnp.pad(xs, ((0, 0), (0, HGRN_CHUNK - ss), (0, 0)))
    xs, s_s = _hgrn_layer(xs, state_hgrn[0], **hg, tile=HGRN_CHUNK, valid_rows=ss)
    y_sample = peer(xs[:, :ss].reshape(bs * ss, D_MODEL), 1, bs * ss, bs * ss).reshape(bs, ss, D_MODEL)

    kv5 = lambda a, b: a.reshape(1, b, WINDOW, N_KV_HEADS, HEAD_DIM)
    return (y_prompt, y_sample, conv_p[None], kv5(k_p, bp), kv5(v_p, bp), s_p[None],
            conv_s[None], kv5(k_s, bs), kv5(v_s, bs), s_s[None])
```

```python
import functools

import jax
import jax.numpy as jnp
import numpy as np
from jax import lax
from jax.experimental import pallas as pl
from jax.experimental.pallas import tpu as pltpu

F32 = jnp.float32
BF16 = jnp.bfloat16

D_MODEL = 1024
RMS_EPS = 1e-6
CHUNK = 64
WINDOW = 128
CONV_DIM = 512
N_Q_HEADS = 8
N_KV_HEADS = 2
HEAD_DIM = 64
ATTN_DIM = 512
KV_DIM = 128
EVEN_IN_DIM = 2304
HGRN_HEADS = 8
HGRN_DK = 128
HGRN_BLOCK = 16
HGRN_CHUNK = 128
PEER_HEADS = 8
PEER_NKEYS = 128
PEER_TOPK = 16
PEER_N_EXPERTS = PEER_NKEYS * PEER_NKEYS
PEER_EXPERT_CHUNK = 2048
PEER_PACK_BLOCK = 1024
LANES = 128
BF16_ROWS = 16
MXU_DIM = 256

VMEM_LIMIT_BYTES = 52 * 1024 * 1024

NEG_INF = float("-inf")
POS_INF = float("inf")


def _rms_scale(x):
    return lax.rsqrt(jnp.mean(x * x, axis=-1, keepdims=True) + RMS_EPS)


def _split_dot(x, w_bf16):
    hi = x.astype(BF16)
    lo = (x - hi.astype(F32)).astype(BF16)
    return (jnp.dot(hi, w_bf16, preferred_element_type=F32)
            + jnp.dot(lo, w_bf16, preferred_element_type=F32))


def _even_kernel(x_ref, conv0_ref, kc0_ref, vc0_ref, g_ref, win_ref, convw_ref,
                 qg_ref, kg_ref, sink_ref, hsum_q_ref, hsum_k_ref, wout_ref,
                 y_ref, nconv_ref, nk_ref, nv_ref,
                 u_scr, k_scr, v_scr, mix_scr,
                 *, tile, valid_rows, has_cache):
    t = pl.program_id(1)

    @pl.when(t == 0)
    def _():
        u_scr[0:8, :] = jnp.zeros((8, CONV_DIM), F32)
        u_scr[6:8, :] = conv0_ref[0]
        k_scr[0:WINDOW, :] = kc0_ref[0]
        v_scr[0:WINDOW, :] = vc0_ref[0]

    x = x_ref[0]
    xn = x * _rms_scale(x) * g_ref[...]
    z = jnp.dot(xn.astype(BF16), win_ref[...], preferred_element_type=F32)
    bg = z[:, 0:512]
    cg = z[:, 512:1024]
    hh = z[:, 1024:1536]
    q = z[:, 1536:2048]
    k = z[:, 2048:2176]
    v = z[:, 2176:2304]

    u = cg * hh
    u_scr[8:8 + tile, :] = u
    cw = convw_ref[...]
    conv = (cw[0:1, :] * u_scr[6:6 + tile, :] + cw[1:2, :] * u_scr[7:7 + tile, :]
            + cw[2:3, :] * u)
    mix_scr[:, 0:CONV_DIM] = bg * conv
    tail = u_scr[6 + valid_rows:8 + valid_rows, :]
    nconv_ref[0] = tail
    u_scr[6:8, :] = tail

    q_ms = _split_dot(q * q, hsum_q_ref[...]) * (1.0 / HEAD_DIM)
    q = q * lax.rsqrt(q_ms + RMS_EPS) * qg_ref[...] * (HEAD_DIM ** -0.5)
    k_ms = _split_dot(k * k, hsum_k_ref[...]) * (1.0 / HEAD_DIM)
    k = k * lax.rsqrt(k_ms + RMS_EPS) * kg_ref[...]
    k_scr[WINDOW:WINDOW + tile, :] = k
    v_scr[WINDOW:WINDOW + tile, :] = v

    lane = lax.broadcasted_iota(jnp.int32, (CHUNK, KV_DIM), 1)
    low_half = lane < HEAD_DIM
    sink = sink_ref[...]
    nkeys = WINDOW + CHUNK
    col = lax.broadcasted_iota(jnp.int32, (N_Q_HEADS * CHUNK, nkeys), 1)
    for j in range(tile // CHUNK):
        r0 = j * CHUNK
        blocks = []
        for b in range(N_Q_HEADS):
            qv = q[r0:r0 + CHUNK, (b % 4) * KV_DIM:(b % 4 + 1) * KV_DIM]
            keep = low_half if b < 4 else jnp.logical_not(low_half)
            blocks.append(jnp.where(keep, qv, 0.0))
        qs = jnp.concatenate(blocks, axis=0).astype(BF16)
        kw = k_scr[r0:r0 + nkeys, :].astype(BF16)
        vw = v_scr[r0:r0 + nkeys, :].astype(BF16)
        s = lax.dot_general(qs, kw, (((1,), (1,)), ((), ())),
                            preferred_element_type=F32)
        if valid_rows < tile:
            s = jnp.where(col < WINDOW + valid_rows, s, NEG_INF)
        if not has_cache and r0 < WINDOW:
            s = jnp.where(jnp.logical_or(col >= WINDOW - r0, t > 0), s, NEG_INF)
        m = jnp.maximum(jnp.max(s, axis=-1, keepdims=True), sink)
        p = jnp.exp(s - m)
        p = p / (jnp.sum(p, axis=-1, keepdims=True) + jnp.exp(sink - m))
        o = jnp.dot(p.astype(BF16), vw, preferred_element_type=F32)
        for jj in range(4):
            oj = jnp.where(low_half, o[jj * CHUNK:(jj + 1) * CHUNK, :],
                           o[(4 + jj) * CHUNK:(5 + jj) * CHUNK, :])
            mix_scr[r0:r0 + CHUNK, CONV_DIM + jj * KV_DIM:CONV_DIM + (jj + 1) * KV_DIM] = oj

    nk = k_scr[valid_rows:valid_rows + WINDOW, :]
    nv = v_scr[valid_rows:valid_rows + WINDOW, :]
    nk_ref[0] = nk
    nv_ref[0] = nv
    k_scr[0:WINDOW, :] = nk
    v_scr[0:WINDOW, :] = nv

    mix = jnp.dot(mix_scr[...].astype(BF16), wout_ref[...], preferred_element_type=F32)
    y_ref[0] = x + mix


def _even_layer(x, conv0, kc0, vc0, gain, win, convw, qg, kg, sink_rows, hsum_q, hsum_k, wout,
                *, tile, valid_rows, has_cache):
    b, s, _ = x.shape
    nt = s // tile
    const2 = lambda i, j: (0, 0)
    per_b = lambda i, j: (i, 0, 0)
    kern = functools.partial(_even_kernel, tile=tile, valid_rows=valid_rows, has_cache=has_cache)
    return pl.pallas_call(
        kern,
        grid=(b, nt),
        in_specs=[
            pl.BlockSpec((1, tile, D_MODEL), lambda i, j: (i, j, 0)),
            pl.BlockSpec((1, 2, CONV_DIM), per_b),
            pl.BlockSpec((1, WINDOW, KV_DIM), per_b),
            pl.BlockSpec((1, WINDOW, KV_DIM), per_b),
            pl.BlockSpec((1, D_MODEL), const2),
            pl.BlockSpec((D_MODEL, EVEN_IN_DIM), const2),
            pl.BlockSpec((3, CONV_DIM), const2),
            pl.BlockSpec((1, ATTN_DIM), const2),
            pl.BlockSpec((1, KV_DIM), const2),
            pl.BlockSpec((N_Q_HEADS * CHUNK, 1), const2),
            pl.BlockSpec((ATTN_DIM, ATTN_DIM), const2),
            pl.BlockSpec((KV_DIM, KV_DIM), const2),
            pl.BlockSpec((D_MODEL, D_MODEL), const2),
        ],
        out_specs=[
            pl.BlockSpec((1, tile, D_MODEL), lambda i, j: (i, j, 0)),
            pl.BlockSpec((1, 2, CONV_DIM), per_b),
            pl.BlockSpec((1, WINDOW, KV_DIM), per_b),
            pl.BlockSpec((1, WINDOW, KV_DIM), per_b),
        ],
        out_shape=[
            jax.ShapeDtypeStruct((b, s, D_MODEL), F32),
            jax.ShapeDtypeStruct((b, 2, CONV_DIM), F32),
            jax.ShapeDtypeStruct((b, WINDOW, KV_DIM), F32),
            jax.ShapeDtypeStruct((b, WINDOW, KV_DIM), F32),
        ],
        scratch_shapes=[
            pltpu.VMEM((8 + tile, CONV_DIM), F32),
            pltpu.VMEM((WINDOW + tile, KV_DIM), F32),
            pltpu.VMEM((WINDOW + tile, KV_DIM), F32),
            pltpu.VMEM((tile, D_MODEL), F32),
        ],
        compiler_params=pltpu.CompilerParams(
            dimension_semantics=("arbitrary", "arbitrary"),
            vmem_limit_bytes=VMEM_LIMIT_BYTES),
        name="even_layer",
    )(x, conv0, kc0, vc0, gain, win, convw, qg, kg, sink_rows, hsum_q, hsum_k, wout)


def _hgrn_kernel(x_ref, s0_ref, g_ref, win_ref, lb_ref, og_ref, wout_ref,
                 y_ref, snew_ref,
                 s_scr, q_scr, k_scr, v_scr, lf_scr, o_scr,
                 *, tile, valid_rows):
    t = pl.program_id(1)

    @pl.when(t == 0)
    def _():
        s_scr[...] = s0_ref[0]

    x = x_ref[0]
    xn = x * _rms_scale(x) * g_ref[...]
    z = jnp.dot(xn.astype(BF16), win_ref[...], preferred_element_type=F32)
    lb = lb_ref[...]
    fg = lb + (1.0 - lb) * jax.nn.sigmoid(z[:, D_MODEL:2 * D_MODEL])
    logf = jnp.log(fg)
    kk = 1.0 - fg
    if valid_rows < tile:
        row = lax.broadcasted_iota(jnp.int32, (tile, D_MODEL), 0)
        live = row < valid_rows
        logf = jnp.where(live, logf, 0.0)
        kk = jnp.where(live, kk, 0.0)
    q_scr[...] = z[:, 0:D_MODEL]
    k_scr[...] = kk
    v_scr[...] = z[:, 2 * D_MODEL:3 * D_MODEL]
    gate = z[:, 3 * D_MODEL:4 * D_MODEL]

    ri = lax.broadcasted_iota(jnp.int32, (tile, tile), 0)
    ci = lax.broadcasted_iota(jnp.int32, (tile, tile), 1)
    same = (ri // HGRN_CHUNK) == (ci // HGRN_CHUNK)
    tril = jnp.where(jnp.logical_and(same, ci <= ri), 1.0, 0.0).astype(BF16)
    hi = logf.astype(BF16)
    lo = (logf - hi.astype(F32)).astype(BF16)
    lf_scr[...] = (jnp.dot(tril, hi, preferred_element_type=F32)
                   + jnp.dot(tril, lo, preferred_element_type=F32))

    for c in range(tile // HGRN_CHUNK):
        c0 = c * HGRN_CHUNK
        g = lf_scr[c0:c0 + HGRN_CHUNK, :]
        gtot = lf_scr[c0 + HGRN_CHUNK - 1:c0 + HGRN_CHUNK, :]
        qe = (q_scr[c0:c0 + HGRN_CHUNK, :] * jnp.exp(g)).astype(BF16)
        kh = k_scr[c0:c0 + HGRN_CHUNK, :] * jnp.exp(gtot - g)
        kh_t = kh.T.astype(BF16)
        dec_t = jnp.broadcast_to(jnp.exp(gtot), (HGRN_CHUNK, D_MODEL)).T
        vc = v_scr[c0:c0 + HGRN_CHUNK, :].astype(BF16)
        for h in range(HGRN_HEADS):
            hs = slice(h * HGRN_DK, (h + 1) * HGRN_DK)
            s_h = s_scr[h]
            o_scr[c0:c0 + HGRN_CHUNK, hs] = jnp.dot(
                qe[:, hs], s_h.astype(BF16), preferred_element_type=F32)
            s_scr[h] = dec_t[hs, :] * s_h + jnp.dot(
                kh_t[hs, :], vc[:, hs], preferred_element_type=F32)
        for j in range(HGRN_CHUNK // HGRN_BLOCK):
            r0 = c0 + j * HGRN_BLOCK
            nrow = HGRN_CHUNK - j * HGRN_BLOCK
            gj = lf_scr[r0:c0 + HGRN_CHUNK, :]
            if j == 0:
                rel = gj
            else:
                rel = gj - lf_scr[r0 - 1:r0, :]
            qj = (q_scr[r0:c0 + HGRN_CHUNK, :] * jnp.exp(rel)).astype(BF16)
            kj = (k_scr[r0:r0 + HGRN_BLOCK, :] * jnp.exp(-rel[0:HGRN_BLOCK, :])).astype(BF16)
            vj = v_scr[r0:r0 + HGRN_BLOCK, :].astype(BF16)
            causal = (lax.broadcasted_iota(jnp.int32, (nrow, HGRN_BLOCK), 0)
                      >= lax.broadcasted_iota(jnp.int32, (nrow, HGRN_BLOCK), 1))
            for h in range(HGRN_HEADS):
                hs = slice(h * HGRN_DK, (h + 1) * HGRN_DK)
                a = lax.dot_general(qj[:, hs], kj[:, hs], (((1,), (1,)), ((), ())),
                                    preferred_element_type=F32)
                a = jnp.where(causal, a, 0.0).astype(BF16)
                o_scr[r0:c0 + HGRN_CHUNK, hs] += jnp.dot(
                    a, vj[:, hs], preferred_element_type=F32)

    snew_ref[0] = s_scr[...]
    o = o_scr[...]
    o = o * _rms_scale(o) * og_ref[...]
    o = o * (gate * jax.nn.sigmoid(gate))
    y_ref[0] = x + jnp.dot(o.astype(BF16), wout_ref[...], preferred_element_type=F32)


def _hgrn_layer(x, s0, gain, win, lb, out_gain, wout, *, tile, valid_rows):
    b, s, _ = x.shape
    nt = s // tile
    const2 = lambda i, j: (0, 0)
    kern = functools.partial(_hgrn_kernel, tile=tile, valid_rows=valid_rows)
    return pl.pallas_call(
        kern,
        grid=(b, nt),
        in_specs=[
            pl.BlockSpec((1, tile, D_MODEL), lambda i, j: (i, j, 0)),
            pl.BlockSpec((1, HGRN_HEADS, HGRN_DK, HGRN_DK), lambda i, j: (i, 0, 0, 0)),
            pl.BlockSpec((1, D_MODEL), const2),
            pl.BlockSpec((D_MODEL, 4 * D_MODEL), const2),
            pl.BlockSpec((1, D_MODEL), const2),
            pl.BlockSpec((1, D_MODEL), const2),
            pl.BlockSpec((D_MODEL, D_MODEL), const2),
        ],
        out_specs=[
            pl.BlockSpec((1, tile, D_MODEL), lambda i, j: (i, j, 0)),
            pl.BlockSpec((1, HGRN_HEADS, HGRN_DK, HGRN_DK), lambda i, j: (i, 0, 0, 0)),
        ],
        out_shape=[
            jax.ShapeDtypeStruct((b, s, D_MODEL), F32),
            jax.ShapeDtypeStruct((b, HGRN_HEADS, HGRN_DK, HGRN_DK), F32),
        ],
        scratch_shapes=[
            pltpu.VMEM((HGRN_HEADS, HGRN_DK, HGRN_DK), F32),
            pltpu.VMEM((tile, D_MODEL), F32),
            pltpu.VMEM((tile, D_MODEL), F32),
            pltpu.VMEM((tile, D_MODEL), F32),
            pltpu.VMEM((tile, D_MODEL), F32),
            pltpu.VMEM((tile, D_MODEL), F32),
        ],
        compiler_params=pltpu.CompilerParams(
            dimension_semantics=("arbitrary", "arbitrary"),
            vmem_limit_bytes=VMEM_LIMIT_BYTES),
        name="hgrn_layer",
    )(x, s0, gain, win, lb, out_gain, wout)


def _sort_network(n):
    pairs = []

    def merge(lo, hi, r):
        step = r * 2
        if step < hi - lo:
            merge(lo, hi, step)
            merge(lo + r, hi, step)
            pairs.extend((i, i + r) for i in range(lo + r, hi - r, step))
        else:
            pairs.append((lo, lo + r))

    def sort(lo, hi):
        if hi - lo >= 1:
            mid = lo + (hi - lo) // 2
            sort(lo, mid)
            sort(mid + 1, hi)
            merge(lo, hi, 1)

    sort(0, n - 1)
    return pairs


_SORT16 = _sort_network(PEER_TOPK)


def _sort16_desc(v):
    v = list(v)
    for i, j in _SORT16:
        v[i], v[j] = jnp.maximum(v[i], v[j]), jnp.minimum(v[i], v[j])
    return v


def _merge_top16(a, b):
    c = [jnp.maximum(a[i], b[PEER_TOPK - 1 - i]) for i in range(PEER_TOPK)]
    for d in (8, 4, 2, 1):
        for i in range(PEER_TOPK):
            if not i & d:
                c[i], c[i + d] = jnp.maximum(c[i], c[i + d]), jnp.minimum(c[i], c[i + d])
    return c


def _top16_sorted(load_slab):
    lists = [_sort16_desc([load_slab(PEER_TOPK * g + i) for i in range(PEER_TOPK)])
             for g in range(PEER_NKEYS // PEER_TOPK)]
    while len(lists) > 1:
        lists = [_merge_top16(a, b) for a, b in zip(lists[0::2], lists[1::2])]
    return lists[0]


def _bf16_pair_words(x):
    bits = lax.bitcast_convert_type(x.astype(BF16).astype(F32), jnp.int32)
    return bits | lax.shift_right_logical(bits, 16)


def _route_kernel(x_ref, g_ref, wq_ref, kb_ref,
                  xnt_ref, q2h_ref, e2h_ref, thr_ref, e1_ref,
                  s1_scr, s2_scr, a_scr, b_scr, cand_scr,
                  *, tile):
    x = x_ref[...]
    xn = x * _rms_scale(x) * g_ref[...]
    xnt = xn.T.astype(BF16)
    xnt_ref[...] = pltpu.bitcast(xnt, jnp.int32)
    qt = jnp.dot(wq_ref[...], xnt, preferred_element_type=F32).astype(BF16)
    half = PEER_HEADS * 128
    s1 = jnp.dot(kb_ref[0], qt[0:half, :], preferred_element_type=F32)
    s2 = jnp.dot(kb_ref[1], qt[half:2 * half, :], preferred_element_type=F32)
    s1_scr[...] = s1.reshape(PEER_NKEYS, PEER_HEADS, tile)
    s2_scr[...] = s2.reshape(PEER_NKEYS, PEER_HEADS, tile)
    for lt in range(tile // LANES):
        ls = slice(lt * LANES, (lt + 1) * LANES)
        for src, dst in ((s1_scr, a_scr), (s2_scr, b_scr)):
            top = _top16_sorted(lambda kidx, src=src: src[kidx, :, ls])
            for i in range(PEER_TOPK):
                dst[i, :, ls] = top[i]

    pairs = [(i, j) for i in range(PEER_TOPK) for j in range(PEER_TOPK)
             if (i + 1) * (j + 1) <= PEER_TOPK]
    rows = [[] for _ in range(PEER_TOPK)]
    for n, (i, j) in enumerate(pairs):
        c = a_scr[i] + b_scr[j]
        cand_scr[n] = c
        rows[i].append(c)
    s_a = _sort16_desc(rows[1] + [rows[i][0] for i in range(8, PEER_TOPK)])
    s_b = _sort16_desc(rows[2] + rows[3] + rows[4] + rows[5] + rows[6])
    top = _merge_top16(_merge_top16(rows[0], s_a), s_b)
    tau = jnp.minimum(top[13], jnp.minimum(jnp.maximum(top[14], rows[7][1]),
                                           jnp.maximum(top[15], rows[7][0])))

    a0 = a_scr[0]
    b0 = b_scr[0]
    zsum = jnp.zeros((PEER_HEADS, tile), F32)
    codes = []
    for i in range(PEER_TOPK):
        lam = jnp.zeros((PEER_HEADS, tile), F32)
        ea = jnp.exp(a_scr[i] - a0)
        for n, (pi, pj) in enumerate(pairs):
            if pi != i:
                continue
            sel = cand_scr[n] >= tau
            lam = lam + jnp.where(sel, 1.0, 0.0)
            zsum = zsum + jnp.where(sel, ea * jnp.exp(b_scr[pj] - b0), 0.0)
        codes.append((PEER_TOPK + 1.0) - lam)

    s1 = s1_scr[...]
    thr = jnp.full((PEER_NKEYS, PEER_HEADS, tile), PEER_TOPK + 1.0, F32)
    for i in range(PEER_TOPK):
        thr = jnp.where(s1 == a_scr[i][None], codes[i][None], thr)
    thr_w = _bf16_pair_words(thr).reshape(PEER_NKEYS * PEER_HEADS, tile)
    e1_w = _bf16_pair_words(jnp.exp(s1 - a0[None])).reshape(PEER_NKEYS * PEER_HEADS, tile)
    for lt in range(tile // LANES):
        thr_ref[lt] = thr_w[:, lt * LANES:(lt + 1) * LANES]
        e1_ref[lt] = e1_w[:, lt * LANES:(lt + 1) * LANES]

    s2hm = jnp.dot(kb_ref[2], qt[half:2 * half, :], preferred_element_type=F32)
    zscale = 0.5 / zsum
    for h in range(PEER_HEADS):
        s2h = s2hm[h * PEER_NKEYS:(h + 1) * PEER_NKEYS, :]
        q2 = jnp.zeros((PEER_NKEYS, tile), F32)
        for jj in range(PEER_TOPK - 1, -1, -1):
            q2 = jnp.where(s2h >= b_scr[jj, h:h + 1, :], float(PEER_TOPK - jj), q2)
        q2h_ref[h] = pltpu.bitcast(q2.astype(BF16), jnp.int32)
        e2 = jnp.where(q2 > 0.0, jnp.exp(s2h - b0[h:h + 1, :]) * zscale[h:h + 1, :], 0.0)
        e2h_ref[h] = pltpu.bitcast(e2.astype(BF16), jnp.int32)


def _peer_route(x, gain, wq_t, kbig, *, tile):
    n = x.shape[0]
    nt = n // tile
    npairs = sum(1 for i in range(PEER_TOPK) for j in range(PEER_TOPK)
                 if (i + 1) * (j + 1) <= PEER_TOPK)
    rows = PEER_NKEYS * PEER_HEADS
    kern = functools.partial(_route_kernel, tile=tile)
    return pl.pallas_call(
        kern,
        grid=(nt,),
        in_specs=[
            pl.BlockSpec((tile, D_MODEL), lambda i: (i, 0)),
            pl.BlockSpec((1, D_MODEL), lambda i: (0, 0)),
            pl.BlockSpec((2 * rows, D_MODEL), lambda i: (0, 0)),
            pl.BlockSpec((3, rows, rows), lambda i: (0, 0, 0)),
        ],
        out_specs=[
            pl.BlockSpec((D_MODEL // 2, tile), lambda i: (0, i)),
            pl.BlockSpec((PEER_HEADS, PEER_NKEYS // 2, tile), lambda i: (0, 0, i)),
            pl.BlockSpec((PEER_HEADS, PEER_NKEYS // 2, tile), lambda i: (0, 0, i)),
            pl.BlockSpec((tile // LANES, rows, LANES), lambda i: (i, 0, 0)),
            pl.BlockSpec((tile // LANES, rows, LANES), lambda i: (i, 0, 0)),
        ],
        out_shape=[
            jax.ShapeDtypeStruct((D_MODEL // 2, n), jnp.int32),
            jax.ShapeDtypeStruct((PEER_HEADS, PEER_NKEYS // 2, n), jnp.int32),
            jax.ShapeDtypeStruct((PEER_HEADS, PEER_NKEYS // 2, n), jnp.int32),
            jax.ShapeDtypeStruct((n // LANES, rows, LANES), jnp.int32),
            jax.ShapeDtypeStruct((n // LANES, rows, LANES), jnp.int32),
        ],
        scratch_shapes=[
            pltpu.VMEM((PEER_NKEYS, PEER_HEADS, tile), F32),
            pltpu.VMEM((PEER_NKEYS, PEER_HEADS, tile), F32),
            pltpu.VMEM((PEER_TOPK, PEER_HEADS, tile), F32),
            pltpu.VMEM((PEER_TOPK, PEER_HEADS, tile), F32),
            pltpu.VMEM((npairs, PEER_HEADS, tile), F32),
        ],
        compiler_params=pltpu.CompilerParams(
            dimension_semantics=("arbitrary",),
            vmem_limit_bytes=VMEM_LIMIT_BYTES),
        name="peer_route",
    )(x, gain, wq_t, kbig)


def _dense_kernel(x_ref, xnt_ref, q2h_ref, e2h_ref, thr_ref, e1_ref, u_ref, vt_ref,
                  y_ref, acc_ref, w_scr, *, tile, echunk):
    j = pl.program_id(1)

    @pl.when(j == 0)
    def _():
        acc_ref[...] = jnp.zeros_like(acc_ref)

    mxu_w = min(tile, MXU_DIM)
    ngrp = PEER_NKEYS // BF16_ROWS

    def bcast_row(word_ref, r, l0):
        row = word_ref[l0 // LANES, r:r + 1, :]
        return pltpu.bitcast(jnp.broadcast_to(row, (8, LANES)), BF16)

    h_all = jnp.dot(pltpu.bitcast(u_ref[0], BF16), pltpu.bitcast(xnt_ref[...], BF16),
                         preferred_element_type=F32)
    na = echunk // PEER_NKEYS
    ablk = 4
    gblk = 4
    for mb in range(tile // mxu_w):
        m0 = mb * mxu_w
        for lb in range(mxu_w // LANES):
            l0 = m0 + lb * LANES
            for gb in range(ngrp // gblk):
                grp = [gb * gblk + g for g in range(gblk)]
                for ab in range(na // ablk):
                    keys = [ab * ablk + a for a in range(ablk)]
                    gsum = [[jnp.zeros((BF16_ROWS, LANES), BF16) for _ in grp] for _ in keys]
                    for h in range(PEER_HEADS):
                        q2 = [pltpu.bitcast(q2h_ref[h, 8 * g:8 * g + 8, l0:l0 + LANES], BF16)
                              for g in grp]
                        e2 = [pltpu.bitcast(e2h_ref[h, 8 * g:8 * g + 8, l0:l0 + LANES], BF16)
                              for g in grp]
                        for ai, a in enumerate(keys):
                            thr = bcast_row(thr_ref, a * PEER_HEADS + h, l0)
                            e1 = bcast_row(e1_ref, a * PEER_HEADS + h, l0)
                            for g in range(gblk):
                                hit = q2[g] >= thr
                                gsum[ai][g] = gsum[ai][g] + jnp.where(hit, e2[g], 0.0) * e1
                    for ai, a in enumerate(keys):
                        for g in range(gblk):
                            e0 = a * PEER_NKEYS + grp[g] * BF16_ROWS
                            hv = h_all[e0:e0 + BF16_ROWS, l0:l0 + LANES]
                            act = (hv * (1.0 + lax.erf(hv * (2.0 ** -0.5)))).astype(BF16)
                            w_scr[e0:e0 + BF16_ROWS, l0:l0 + LANES] = (
                                jnp.where(gsum[ai][g] > 0.0, act, 0.0) * gsum[ai][g])
        acc_ref[:, m0:m0 + mxu_w] += jnp.dot(
            pltpu.bitcast(vt_ref[0], BF16), w_scr[:, m0:m0 + mxu_w],
            preferred_element_type=F32)

    @pl.when(j == pl.num_programs(1) - 1)
    def _():
        y_ref[...] = x_ref[...] + acc_ref[...].T


def _peer_dense(x, xnt, q2h, e2h, thr, e1, u_words, vt_words, *, layer, tile, echunk):
    n = x.shape[0]
    nt = n // tile
    nchunk = PEER_N_EXPERTS // echunk
    crows = (echunk // PEER_NKEYS) * PEER_HEADS
    kern = functools.partial(_dense_kernel, tile=tile, echunk=echunk)
    return pl.pallas_call(
        kern,
        grid=(nt, nchunk),
        in_specs=[
            pl.BlockSpec((tile, D_MODEL), lambda i, j: (i, 0)),
            pl.BlockSpec((D_MODEL // 2, tile), lambda i, j: (0, i)),
            pl.BlockSpec((PEER_HEADS, PEER_NKEYS // 2, tile), lambda i, j: (0, 0, i)),
            pl.BlockSpec((PEER_HEADS, PEER_NKEYS // 2, tile), lambda i, j: (0, 0, i)),
            pl.BlockSpec((tile // LANES, crows, LANES), lambda i, j: (i, j, 0)),
            pl.BlockSpec((tile // LANES, crows, LANES), lambda i, j: (i, j, 0)),
            pl.BlockSpec((1, echunk // 2, D_MODEL), lambda i, j: (layer, j, 0)),
            pl.BlockSpec((1, D_MODEL // 2, echunk), lambda i, j: (layer, 0, j)),
        ],
        out_specs=pl.BlockSpec((tile, D_MODEL), lambda i, j: (i, 0)),
        out_shape=jax.ShapeDtypeStruct((n, D_MODEL), F32),
        scratch_shapes=[pltpu.VMEM((D_MODEL, tile), F32),
                        pltpu.VMEM((echunk, tile), BF16)],
        compiler_params=pltpu.CompilerParams(
            dimension_semantics=("arbitrary", "arbitrary"),
            vmem_limit_bytes=VMEM_LIMIT_BYTES),
        name="peer_dense",
    )(x, xnt, q2h, e2h, thr, e1, u_words, vt_words)


def _peer(x, gain, wq_t, kbig, u_words, vt_words, *, layer, route_tile, dense_tile, echunk):
    xnt, q2h, e2h, thr, e1 = _peer_route(x, gain, wq_t, kbig, tile=route_tile)
    return _peer_dense(x, xnt, q2h, e2h, thr, e1, u_words, vt_words, layer=layer,
                       tile=dense_tile, echunk=echunk)


def _even_params(norm_g, w_in, conv_w, q_gain, k_gain, sinks, w_out):
    qcols = np.array([1536 + (j + 4 * hf) * HEAD_DIM + d
                      for j in range(4) for hf in range(2) for d in range(HEAD_DIM)])
    cols = np.concatenate([np.arange(1536), qcols, np.arange(2048, EVEN_IN_DIM)])
    orow = np.array([CONV_DIM + (j + 4 * hf) * HEAD_DIM + d
                     for j in range(4) for hf in range(2) for d in range(HEAD_DIM)])
    rows = np.concatenate([np.arange(CONV_DIM), orow])
    blk = lambda n: jnp.asarray(
        (np.arange(n)[:, None] // HEAD_DIM) == (np.arange(n)[None, :] // HEAD_DIM)).astype(BF16)
    return dict(
        gain=norm_g.reshape(1, D_MODEL),
        win=w_in[:, cols].astype(BF16),
        convw=conv_w,
        qg=jnp.tile(q_gain, N_Q_HEADS).reshape(1, ATTN_DIM),
        kg=jnp.tile(k_gain, N_KV_HEADS).reshape(1, KV_DIM),
        sink_rows=jnp.repeat(sinks, CHUNK).reshape(N_Q_HEADS * CHUNK, 1),
        hsum_q=blk(ATTN_DIM),
        hsum_k=blk(KV_DIM),
        wout=w_out[rows, :].astype(BF16),
    )


def _pack_tables_kernel(u_ref, v_ref, uo_ref, vo_ref):
    uo_ref[0] = pltpu.bitcast(u_ref[0].astype(BF16), jnp.int32)
    vo_ref[0] = pltpu.bitcast(v_ref[0].T.astype(BF16), jnp.int32)


def _pack_tables(u_tab, v_tab, *, eblk):
    nl, ne, d = u_tab.shape
    return pl.pallas_call(
        _pack_tables_kernel,
        grid=(nl, ne // eblk),
        in_specs=[pl.BlockSpec((1, eblk, d), lambda l, i: (l, i, 0)),
                  pl.BlockSpec((1, eblk, d), lambda l, i: (l, i, 0))],
        out_specs=[pl.BlockSpec((1, eblk // 2, d), lambda l, i: (l, i, 0)),
                   pl.BlockSpec((1, d // 2, eblk), lambda l, i: (l, 0, i))],
        out_shape=[jax.ShapeDtypeStruct((nl, ne // 2, d), jnp.int32),
                   jax.ShapeDtypeStruct((nl, d // 2, ne), jnp.int32)],
        compiler_params=pltpu.CompilerParams(
            dimension_semantics=("arbitrary", "arbitrary"),
            vmem_limit_bytes=VMEM_LIMIT_BYTES),
        name="pack_tables",
    )(u_tab, v_tab)


def _peer_params(norm_g, w_query, sub_keys):
    wq_t = w_query.T.reshape(PEER_HEADS, 2, 128, D_MODEL).transpose(1, 0, 2, 3)
    wq_t = wq_t.reshape(2 * PEER_HEADS * 128, D_MODEL).astype(BF16)
    eye = jnp.eye(PEER_HEADS, dtype=sub_keys.dtype)
    kbig = jnp.einsum('hpkd,hg->pkhgd', sub_keys, eye).reshape(
        2, PEER_NKEYS * PEER_HEADS, PEER_HEADS * 128)
    khm = jnp.einsum('hkd,hg->hkgd', sub_keys[:, 1], eye).reshape(
        1, PEER_HEADS * PEER_NKEYS, PEER_HEADS * 128)
    kbig = jnp.concatenate([kbig, khm], axis=0).astype(BF16)
    return dict(gain=norm_g.reshape(1, D_MODEL), wq_t=wq_t, kbig=kbig)


def kernel(x_prompt, x_sample, cache_conv, cache_k, cache_v, state_hgrn, norm_mix, norm_ffn,
           even_w_in, even_conv_w, even_q_gain, even_k_gain, even_sinks, even_w_out,
           hgrn_w_in, hgrn_lb, hgrn_out_gain, hgrn_w_out,
           peer_w_query, peer_sub_keys, peer_u, peer_v):
    bp, sp, _ = x_prompt.shape
    bs, ss, _ = x_sample.shape

    ev = _even_params(norm_mix[0], even_w_in[0], even_conv_w[0], even_q_gain[0],
                      even_k_gain[0], even_sinks[0], even_w_out[0])
    u_words, vt_words = _pack_tables(peer_u, peer_v, eblk=PEER_PACK_BLOCK)
    pe = [_peer_params(norm_ffn[l], peer_w_query[l], peer_sub_keys[l]) for l in range(2)]
    lbs = jax.nn.softmax(hgrn_lb.astype(F32), axis=0)
    lbs = jnp.cumsum(lbs, axis=0) - lbs[0]
    hg = dict(gain=norm_mix[1].reshape(1, D_MODEL), win=hgrn_w_in[0].astype(BF16),
              lb=lbs[1].reshape(1, D_MODEL), out_gain=hgrn_out_gain[0].reshape(1, D_MODEL),
              wout=hgrn_w_out[0].astype(BF16))

    def peer(x2d, l, route_tile, dense_tile):
        return _peer(x2d, pe[l]['gain'], pe[l]['wq_t'], pe[l]['kbig'], u_words, vt_words,
                     layer=l, route_tile=route_tile, dense_tile=dense_tile,
                     echunk=PEER_EXPERT_CHUNK)

    zc = jnp.zeros((bp, 2, CONV_DIM), F32)
    zkv = jnp.zeros((bp, WINDOW, KV_DIM), F32)
    x, conv_p, k_p, v_p = _even_layer(x_prompt, zc, zkv, zkv, **ev,
                                      tile=256, valid_rows=256, has_cache=False)
    x = peer(x.reshape(bp * sp, D_MODEL), 0, 256, 512).reshape(bp, sp, D_MODEL)
    s0 = jnp.zeros((bp, HGRN_HEADS, HGRN_DK, HGRN_DK), F32)
    x, s_p = _hgrn_layer(x, s0, **hg, tile=256, valid_rows=256)
    y_prompt = peer(x.reshape(bp * sp, D_MODEL), 1, 256, 512).reshape(bp, sp, D_MODEL)

    xs = jnp.pad(x_sample, ((0, 0), (0, CHUNK - ss), (0, 0)))
    xs, conv_s, k_s, v_s = _even_layer(
        xs, cache_conv[0], cache_k[0].reshape(bs, WINDOW, KV_DIM),
        cache_v[0].reshape(bs, WINDOW, KV_DIM), **ev, tile=CHUNK, valid_rows=ss, has_cache=True)
    xs = peer(xs[:, :ss].reshape(bs * ss, D_MODEL), 0, bs * ss, bs * ss).reshape(bs, ss, D_MODEL)
    xs = jnp.pad(xs, ((0, 0), (0, HGRN_CHUNK - ss), (0, 0)))
    xs, s_s = _hgrn_layer(xs, state_hgrn[0], **hg, tile=HGRN_CHUNK, valid_rows=ss)
    y_sample = peer(xs[:, :ss].reshape(bs * ss, D_MODEL), 1, bs * ss, bs * ss).reshape(bs, ss, D_MODEL)

    kv5 = lambda a, b: a.reshape(1, b, WINDOW, N_KV_HEADS, HEAD_DIM)
    return (y_prompt, y_sample, conv_p[None], kv5(k_p, bp), kv5(v_p, bp), s_p[None],
            conv_s[None], kv5(k_s, bs), kv5(v_s, bs), s_s[None])
```

```python
import functools

import jax
import jax.numpy as jnp
import numpy as np
from jax import lax
from jax.experimental import pallas as pl
from jax.experimental.pallas import tpu as pltpu

F32 = jnp.float32
BF16 = jnp.bfloat16

D_MODEL = 1024
RMS_EPS = 1e-6
CHUNK = 64
WINDOW = 128
CONV_DIM = 512
N_Q_HEADS = 8
N_KV_HEADS = 2
HEAD_DIM = 64
ATTN_DIM = 512
KV_DIM = 128
EVEN_IN_DIM = 2304
HGRN_HEADS = 8
HGRN_DK = 128
HGRN_BLOCK = 16
HGRN_CHUNK = 128
PEER_HEADS = 8
PEER_NKEYS = 128
PEER_TOPK = 16
PEER_N_EXPERTS = PEER_NKEYS * PEER_NKEYS
LANES = 128
BF16_ROWS = 16
MXU_DIM = 256

MIXER_ROW_TILE = 256
PEER_ROUTE_TILE = 256
PEER_DENSE_TILE = 512
PEER_EXPERT_CHUNK = 2048
PEER_PACK_BLOCK = 1024

VMEM_LIMIT_BYTES = 52 * 1024 * 1024

NEG_INF = float("-inf")


def _rms_scale(x):
    return lax.rsqrt(jnp.mean(x * x, axis=-1, keepdims=True) + RMS_EPS)


def _split_dot(x, w_bf16):
    hi = x.astype(BF16)
    lo = (x - hi.astype(F32)).astype(BF16)
    return (jnp.dot(hi, w_bf16, preferred_element_type=F32)
            + jnp.dot(lo, w_bf16, preferred_element_type=F32))


def _even_kernel(x_ref, conv0_ref, kc0_ref, vc0_ref, g_ref, win_ref, convw_ref,
                 qg_ref, kg_ref, sink_ref, hsum_q_ref, hsum_k_ref, wout_ref,
                 y_ref, nconv_ref, nk_ref, nv_ref,
                 u_scr, k_scr, v_scr, mix_scr,
                 *, tile, valid_rows, has_cache):
    t = pl.program_id(1)

    @pl.when(t == 0)
    def _():
        u_scr[0:8, :] = jnp.zeros((8, CONV_DIM), F32)
        u_scr[6:8, :] = conv0_ref[0]
        k_scr[0:WINDOW, :] = kc0_ref[0]
        v_scr[0:WINDOW, :] = vc0_ref[0]

    x = x_ref[0]
    xn = x * _rms_scale(x) * g_ref[...]
    z = jnp.dot(xn.astype(BF16), win_ref[...], preferred_element_type=F32)
    bg = z[:, 0:512]
    cg = z[:, 512:1024]
    hh = z[:, 1024:1536]
    q = z[:, 1536:2048]
    k = z[:, 2048:2176]
    v = z[:, 2176:2304]

    u = cg * hh
    u_scr[8:8 + tile, :] = u
    cw = convw_ref[...]
    conv = (cw[0:1, :] * u_scr[6:6 + tile, :] + cw[1:2, :] * u_scr[7:7 + tile, :]
            + cw[2:3, :] * u)
    mix_scr[:, 0:CONV_DIM] = bg * conv
    tail = u_scr[6 + valid_rows:8 + valid_rows, :]
    nconv_ref[0] = tail
    u_scr[6:8, :] = tail

    q_ms = _split_dot(q * q, hsum_q_ref[...]) * (1.0 / HEAD_DIM)
    q = q * lax.rsqrt(q_ms + RMS_EPS) * qg_ref[...] * (HEAD_DIM ** -0.5)
    k_ms = _split_dot(k * k, hsum_k_ref[...]) * (1.0 / HEAD_DIM)
    k = k * lax.rsqrt(k_ms + RMS_EPS) * kg_ref[...]
    k_scr[WINDOW:WINDOW + tile, :] = k
    v_scr[WINDOW:WINDOW + tile, :] = v

    lane = lax.broadcasted_iota(jnp.int32, (CHUNK, KV_DIM), 1)
    low_half = lane < HEAD_DIM
    sink = sink_ref[...]
    nkeys = WINDOW + CHUNK
    col = lax.broadcasted_iota(jnp.int32, (N_Q_HEADS * CHUNK, nkeys), 1)
    for j in range(tile // CHUNK):
        r0 = j * CHUNK
        blocks = []
        for b in range(N_Q_HEADS):
            qv = q[r0:r0 + CHUNK, (b % 4) * KV_DIM:(b % 4 + 1) * KV_DIM]
            keep = low_half if b < 4 else jnp.logical_not(low_half)
            blocks.append(jnp.where(keep, qv, 0.0))
        qs = jnp.concatenate(blocks, axis=0).astype(BF16)
        kw = k_scr[r0:r0 + nkeys, :].astype(BF16)
        vw = v_scr[r0:r0 + nkeys, :].astype(BF16)
        s = lax.dot_general(qs, kw, (((1,), (1,)), ((), ())),
                            preferred_element_type=F32)
        if valid_rows < tile:
            s = jnp.where(col < WINDOW + valid_rows, s, NEG_INF)
        if not has_cache and r0 < WINDOW:
            s = jnp.where(jnp.logical_or(col >= WINDOW - r0, t > 0), s, NEG_INF)
        m = jnp.maximum(jnp.max(s, axis=-1, keepdims=True), sink)
        p = jnp.exp(s - m)
        p = p / (jnp.sum(p, axis=-1, keepdims=True) + jnp.exp(sink - m))
        o = jnp.dot(p.astype(BF16), vw, preferred_element_type=F32)
        for jj in range(4):
            oj = jnp.where(low_half, o[jj * CHUNK:(jj + 1) * CHUNK, :],
                           o[(4 + jj) * CHUNK:(5 + jj) * CHUNK, :])
            mix_scr[r0:r0 + CHUNK, CONV_DIM + jj * KV_DIM:CONV_DIM + (jj + 1) * KV_DIM] = oj

    nk = k_scr[valid_rows:valid_rows + WINDOW, :]
    nv = v_scr[valid_rows:valid_rows + WINDOW, :]
    nk_ref[0] = nk
    nv_ref[0] = nv
    k_scr[0:WINDOW, :] = nk
    v_scr[0:WINDOW, :] = nv

    mix = jnp.dot(mix_scr[...].astype(BF16), wout_ref[...], preferred_element_type=F32)
    y_ref[0] = x + mix


def _even_layer(x, conv0, kc0, vc0, gain, win, convw, qg, kg, sink_rows, hsum_q, hsum_k, wout,
                *, tile, valid_rows, has_cache):
    b, s, _ = x.shape
    nt = s // tile
    const2 = lambda i, j: (0, 0)
    per_b = lambda i, j: (i, 0, 0)
    kern = functools.partial(_even_kernel, tile=tile, valid_rows=valid_rows, has_cache=has_cache)
    return pl.pallas_call(
        kern,
        grid=(b, nt),
        in_specs=[
            pl.BlockSpec((1, tile, D_MODEL), lambda i, j: (i, j, 0)),
            pl.BlockSpec((1, 2, CONV_DIM), per_b),
            pl.BlockSpec((1, WINDOW, KV_DIM), per_b),
            pl.BlockSpec((1, WINDOW, KV_DIM), per_b),
            pl.BlockSpec((1, D_MODEL), const2),
            pl.BlockSpec((D_MODEL, EVEN_IN_DIM), const2),
            pl.BlockSpec((3, CONV_DIM), const2),
            pl.BlockSpec((1, ATTN_DIM), const2),
            pl.BlockSpec((1, KV_DIM), const2),
            pl.BlockSpec((N_Q_HEADS * CHUNK, 1), const2),
            pl.BlockSpec((ATTN_DIM, ATTN_DIM), const2),
            pl.BlockSpec((KV_DIM, KV_DIM), const2),
            pl.BlockSpec((D_MODEL, D_MODEL), const2),
        ],
        out_specs=[
            pl.BlockSpec((1, tile, D_MODEL), lambda i, j: (i, j, 0)),
            pl.BlockSpec((1, 2, CONV_DIM), per_b),
            pl.BlockSpec((1, WINDOW, KV_DIM), per_b),
            pl.BlockSpec((1, WINDOW, KV_DIM), per_b),
        ],
        out_shape=[
            jax.ShapeDtypeStruct((b, s, D_MODEL), F32),
            jax.ShapeDtypeStruct((b, 2, CONV_DIM), F32),
            jax.ShapeDtypeStruct((b, WINDOW, KV_DIM), F32),
            jax.ShapeDtypeStruct((b, WINDOW, KV_DIM), F32),
        ],
        scratch_shapes=[
            pltpu.VMEM((8 + tile, CONV_DIM), F32),
            pltpu.VMEM((WINDOW + tile, KV_DIM), F32),
            pltpu.VMEM((WINDOW + tile, KV_DIM), F32),
            pltpu.VMEM((tile, D_MODEL), F32),
        ],
        compiler_params=pltpu.CompilerParams(
            dimension_semantics=("arbitrary", "arbitrary"),
            vmem_limit_bytes=VMEM_LIMIT_BYTES),
        name="even_layer",
    )(x, conv0, kc0, vc0, gain, win, convw, qg, kg, sink_rows, hsum_q, hsum_k, wout)


def _hgrn_kernel(x_ref, s0_ref, g_ref, win_ref, lb_ref, og_ref, wout_ref,
                 y_ref, snew_ref,
                 s_scr, q_scr, k_scr, v_scr, lf_scr, o_scr,
                 *, tile, valid_rows):
    t = pl.program_id(1)

    @pl.when(t == 0)
    def _():
        s_scr[...] = s0_ref[0]

    x = x_ref[0]
    xn = x * _rms_scale(x) * g_ref[...]
    z = jnp.dot(xn.astype(BF16), win_ref[...], preferred_element_type=F32)
    lb = lb_ref[...]
    fg = lb + (1.0 - lb) * jax.nn.sigmoid(z[:, D_MODEL:2 * D_MODEL])
    logf = jnp.log(fg)
    kk = 1.0 - fg
    if valid_rows < tile:
        row = lax.broadcasted_iota(jnp.int32, (tile, D_MODEL), 0)
        live = row < valid_rows
        logf = jnp.where(live, logf, 0.0)
        kk = jnp.where(live, kk, 0.0)
    q_scr[...] = z[:, 0:D_MODEL]
    k_scr[...] = kk
    v_scr[...] = z[:, 2 * D_MODEL:3 * D_MODEL]
    gate = z[:, 3 * D_MODEL:4 * D_MODEL]

    ri = lax.broadcasted_iota(jnp.int32, (tile, tile), 0)
    ci = lax.broadcasted_iota(jnp.int32, (tile, tile), 1)
    same = (ri // HGRN_CHUNK) == (ci // HGRN_CHUNK)
    tril = jnp.where(jnp.logical_and(same, ci <= ri), 1.0, 0.0).astype(BF16)
    hi = logf.astype(BF16)
    lo = (logf - hi.astype(F32)).astype(BF16)
    lf_scr[...] = (jnp.dot(tril, hi, preferred_element_type=F32)
                   + jnp.dot(tril, lo, preferred_element_type=F32))

    for c in range(tile // HGRN_CHUNK):
        c0 = c * HGRN_CHUNK
        g = lf_scr[c0:c0 + HGRN_CHUNK, :]
        gtot = lf_scr[c0 + HGRN_CHUNK - 1:c0 + HGRN_CHUNK, :]
        qe = (q_scr[c0:c0 + HGRN_CHUNK, :] * jnp.exp(g)).astype(BF16)
        kh = k_scr[c0:c0 + HGRN_CHUNK, :] * jnp.exp(gtot - g)
        kh_t = kh.T.astype(BF16)
        dec_t = jnp.broadcast_to(jnp.exp(gtot), (HGRN_CHUNK, D_MODEL)).T
        vc = v_scr[c0:c0 + HGRN_CHUNK, :].astype(BF16)
        for h in range(HGRN_HEADS):
            hs = slice(h * HGRN_DK, (h + 1) * HGRN_DK)
            s_h = s_scr[h]
            o_scr[c0:c0 + HGRN_CHUNK, hs] = jnp.dot(
                qe[:, hs], s_h.astype(BF16), preferred_element_type=F32)
            s_scr[h] = dec_t[hs, :] * s_h + jnp.dot(
                kh_t[hs, :], vc[:, hs], preferred_element_type=F32)
        for j in range(HGRN_CHUNK // HGRN_BLOCK):
            r0 = c0 + j * HGRN_BLOCK
            nrow = HGRN_CHUNK - j * HGRN_BLOCK
            gj = lf_scr[r0:c0 + HGRN_CHUNK, :]
            if j == 0:
                rel = gj
            else:
                rel = gj - lf_scr[r0 - 1:r0, :]
            qj = (q_scr[r0:c0 + HGRN_CHUNK, :] * jnp.exp(rel)).astype(BF16)
            kj = (k_scr[r0:r0 + HGRN_BLOCK, :] * jnp.exp(-rel[0:HGRN_BLOCK, :])).astype(BF16)
            vj = v_scr[r0:r0 + HGRN_BLOCK, :].astype(BF16)
            causal = (lax.broadcasted_iota(jnp.int32, (nrow, HGRN_BLOCK), 0)
                      >= lax.broadcasted_iota(jnp.int32, (nrow, HGRN_BLOCK), 1))
            for h in range(HGRN_HEADS):
                hs = slice(h * HGRN_DK, (h + 1) * HGRN_DK)
                a = lax.dot_general(qj[:, hs], kj[:, hs], (((1,), (1,)), ((), ())),
                                    preferred_element_type=F32)
                a = jnp.where(causal, a, 0.0).astype(BF16)
                o_scr[r0:c0 + HGRN_CHUNK, hs] += jnp.dot(
                    a, vj[:, hs], preferred_element_type=F32)

    snew_ref[0] = s_scr[...]
    o = o_scr[...]
    o = o * _rms_scale(o) * og_ref[...]
    o = o * (gate * jax.nn.sigmoid(gate))
    y_ref[0] = x + jnp.dot(o.astype(BF16), wout_ref[...], preferred_element_type=F32)


def _hgrn_layer(x, s0, gain, win, lb, out_gain, wout, *, tile, valid_rows):
    b, s, _ = x.shape
    nt = s // tile
    const2 = lambda i, j: (0, 0)
    kern = functools.partial(_hgrn_kernel, tile=tile, valid_rows=valid_rows)
    return pl.pallas_call(
        kern,
        grid=(b, nt),
        in_specs=[
            pl.BlockSpec((1, tile, D_MODEL), lambda i, j: (i, j, 0)),
            pl.BlockSpec((1, HGRN_HEADS, HGRN_DK, HGRN_DK), lambda i, j: (i, 0, 0, 0)),
            pl.BlockSpec((1, D_MODEL), const2),
            pl.BlockSpec((D_MODEL, 4 * D_MODEL), const2),
            pl.BlockSpec((1, D_MODEL), const2),
            pl.BlockSpec((1, D_MODEL), const2),
            pl.BlockSpec((D_MODEL, D_MODEL), const2),
        ],
        out_specs=[
            pl.BlockSpec((1, tile, D_MODEL), lambda i, j: (i, j, 0)),
            pl.BlockSpec((1, HGRN_HEADS, HGRN_DK, HGRN_DK), lambda i, j: (i, 0, 0, 0)),
        ],
        out_shape=[
            jax.ShapeDtypeStruct((b, s, D_MODEL), F32),
            jax.ShapeDtypeStruct((b, HGRN_HEADS, HGRN_DK, HGRN_DK), F32),
        ],
        scratch_shapes=[
            pltpu.VMEM((HGRN_HEADS, HGRN_DK, HGRN_DK), F32),
            pltpu.VMEM((tile, D_MODEL), F32),
            pltpu.VMEM((tile, D_MODEL), F32),
            pltpu.VMEM((tile, D_MODEL), F32),
            pltpu.VMEM((tile, D_MODEL), F32),
            pltpu.VMEM((tile, D_MODEL), F32),
        ],
        compiler_params=pltpu.CompilerParams(
            dimension_semantics=("arbitrary", "arbitrary"),
            vmem_limit_bytes=VMEM_LIMIT_BYTES),
        name="hgrn_layer",
    )(x, s0, gain, win, lb, out_gain, wout)


def _sort_network(n):
    pairs = []

    def merge(lo, hi, r):
        step = r * 2
        if step < hi - lo:
            merge(lo, hi, step)
            merge(lo + r, hi, step)
            pairs.extend((i, i + r) for i in range(lo + r, hi - r, step))
        else:
            pairs.append((lo, lo + r))

    def sort(lo, hi):
        if hi - lo >= 1:
            mid = lo + (hi - lo) // 2
            sort(lo, mid)
            sort(mid + 1, hi)
            merge(lo, hi, 1)

    sort(0, n - 1)
    return pairs


_SORT16 = _sort_network(PEER_TOPK)


def _sort16_desc(v):
    v = list(v)
    for i, j in _SORT16:
        v[i], v[j] = jnp.maximum(v[i], v[j]), jnp.minimum(v[i], v[j])
    return v


def _merge_top16(a, b):
    c = [jnp.maximum(a[i], b[PEER_TOPK - 1 - i]) for i in range(PEER_TOPK)]
    for d in (8, 4, 2, 1):
        for i in range(PEER_TOPK):
            if not i & d:
                c[i], c[i + d] = jnp.maximum(c[i], c[i + d]), jnp.minimum(c[i], c[i + d])
    return c


def _top16_sorted(load_slab):
    lists = [_sort16_desc([load_slab(PEER_TOPK * g + i) for i in range(PEER_TOPK)])
             for g in range(PEER_NKEYS // PEER_TOPK)]
    while len(lists) > 1:
        lists = [_merge_top16(a, b) for a, b in zip(lists[0::2], lists[1::2])]
    return lists[0]


def _bf16_pair_words(x):
    bits = lax.bitcast_convert_type(x.astype(BF16).astype(F32), jnp.int32)
    return bits | lax.shift_right_logical(bits, 16)


def _route_kernel(x_ref, g_ref, wq_ref, kb_ref,
                  xnt_ref, q2h_ref, e2h_ref, thr_ref, e1_ref,
                  s1_scr, s2_scr, a_scr, b_scr, cand_scr,
                  *, tile):
    x = x_ref[...]
    xn = x * _rms_scale(x) * g_ref[...]
    xnt = xn.T.astype(BF16)
    xnt_ref[...] = pltpu.bitcast(xnt, jnp.int32)
    qt = jnp.dot(wq_ref[...], xnt, preferred_element_type=F32).astype(BF16)
    half = PEER_HEADS * 128
    s1 = jnp.dot(kb_ref[0], qt[0:half, :], preferred_element_type=F32)
    s2 = jnp.dot(kb_ref[1], qt[half:2 * half, :], preferred_element_type=F32)
    s1_scr[...] = s1.reshape(PEER_NKEYS, PEER_HEADS, tile)
    s2_scr[...] = s2.reshape(PEER_NKEYS, PEER_HEADS, tile)
    for lt in range(tile // LANES):
        ls = slice(lt * LANES, (lt + 1) * LANES)
        for src, dst in ((s1_scr, a_scr), (s2_scr, b_scr)):
            top = _top16_sorted(lambda kidx, src=src: src[kidx, :, ls])
            for i in range(PEER_TOPK):
                dst[i, :, ls] = top[i]

    pairs = [(i, j) for i in range(PEER_TOPK) for j in range(PEER_TOPK)
             if (i + 1) * (j + 1) <= PEER_TOPK]
    rows = [[] for _ in range(PEER_TOPK)]
    for n, (i, j) in enumerate(pairs):
        c = a_scr[i] + b_scr[j]
        cand_scr[n] = c
        rows[i].append(c)
    s_a = _sort16_desc(rows[1] + [rows[i][0] for i in range(8, PEER_TOPK)])
    s_b = _sort16_desc(rows[2] + rows[3] + rows[4] + rows[5] + rows[6])
    top = _merge_top16(_merge_top16(rows[0], s_a), s_b)
    tau = jnp.minimum(top[13], jnp.minimum(jnp.maximum(top[14], rows[7][1]),
                                           jnp.maximum(top[15], rows[7][0])))

    a0 = a_scr[0]
    b0 = b_scr[0]
    zsum = jnp.zeros((PEER_HEADS, tile), F32)
    codes = []
    for i in range(PEER_TOPK):
        lam = jnp.zeros((PEER_HEADS, tile), F32)
        ea = jnp.exp(a_scr[i] - a0)
        for n, (pi, pj) in enumerate(pairs):
            if pi != i:
                continue
            sel = cand_scr[n] >= tau
            lam = lam + jnp.where(sel, 1.0, 0.0)
            zsum = zsum + jnp.where(sel, ea * jnp.exp(b_scr[pj] - b0), 0.0)
        codes.append((PEER_TOPK + 1.0) - lam)

    s1 = s1_scr[...]
    thr = jnp.full((PEER_NKEYS, PEER_HEADS, tile), PEER_TOPK + 1.0, F32)
    for i in range(PEER_TOPK):
        thr = jnp.where(s1 == a_scr[i][None], codes[i][None], thr)
    thr_w = _bf16_pair_words(thr).reshape(PEER_NKEYS * PEER_HEADS, tile)
    e1_w = _bf16_pair_words(jnp.exp(s1 - a0[None])).reshape(PEER_NKEYS * PEER_HEADS, tile)
    for lt in range(tile // LANES):
        thr_ref[lt] = thr_w[:, lt * LANES:(lt + 1) * LANES]
        e1_ref[lt] = e1_w[:, lt * LANES:(lt + 1) * LANES]

    s2hm = jnp.dot(kb_ref[2], qt[half:2 * half, :], preferred_element_type=F32)
    zscale = 0.5 / zsum
    for h in range(PEER_HEADS):
        s2h = s2hm[h * PEER_NKEYS:(h + 1) * PEER_NKEYS, :]
        q2 = jnp.zeros((PEER_NKEYS, tile), F32)
        for jj in range(PEER_TOPK - 1, -1, -1):
            q2 = jnp.where(s2h >= b_scr[jj, h:h + 1, :], float(PEER_TOPK - jj), q2)
        q2h_ref[h] = pltpu.bitcast(q2.astype(BF16), jnp.int32)
        e2 = jnp.where(q2 > 0.0, jnp.exp(s2h - b0[h:h + 1, :]) * zscale[h:h + 1, :], 0.0)
        e2h_ref[h] = pltpu.bitcast(e2.astype(BF16), jnp.int32)


def _peer_route(x, gain, wq_t, kbig, *, tile):
    n = x.shape[0]
    nt = n // tile
    npairs = sum(1 for i in range(PEER_TOPK) for j in range(PEER_TOPK)
                 if (i + 1) * (j + 1) <= PEER_TOPK)
    rows = PEER_NKEYS * PEER_HEADS
    kern = functools.partial(_route_kernel, tile=tile)
    return pl.pallas_call(
        kern,
        grid=(nt,),
        in_specs=[
            pl.BlockSpec((tile, D_MODEL), lambda i: (i, 0)),
            pl.BlockSpec((1, D_MODEL), lambda i: (0, 0)),
            pl.BlockSpec((2 * rows, D_MODEL), lambda i: (0, 0)),
            pl.BlockSpec((3, rows, rows), lambda i: (0, 0, 0)),
        ],
        out_specs=[
            pl.BlockSpec((D_MODEL // 2, tile), lambda i: (0, i)),
            pl.BlockSpec((PEER_HEADS, PEER_NKEYS // 2, tile), lambda i: (0, 0, i)),
            pl.BlockSpec((PEER_HEADS, PEER_NKEYS // 2, tile), lambda i: (0, 0, i)),
            pl.BlockSpec((tile // LANES, rows, LANES), lambda i: (i, 0, 0)),
            pl.BlockSpec((tile // LANES, rows, LANES), lambda i: (i, 0, 0)),
        ],
        out_shape=[
            jax.ShapeDtypeStruct((D_MODEL // 2, n), jnp.int32),
            jax.ShapeDtypeStruct((PEER_HEADS, PEER_NKEYS // 2, n), jnp.int32),
            jax.ShapeDtypeStruct((PEER_HEADS, PEER_NKEYS // 2, n), jnp.int32),
            jax.ShapeDtypeStruct((n // LANES, rows, LANES), jnp.int32),
            jax.ShapeDtypeStruct((n // LANES, rows, LANES), jnp.int32),
        ],
        scratch_shapes=[
            pltpu.VMEM((PEER_NKEYS, PEER_HEADS, tile), F32),
            pltpu.VMEM((PEER_NKEYS, PEER_HEADS, tile), F32),
            pltpu.VMEM((PEER_TOPK, PEER_HEADS, tile), F32),
            pltpu.VMEM((PEER_TOPK, PEER_HEADS, tile), F32),
            pltpu.VMEM((npairs, PEER_HEADS, tile), F32),
        ],
        compiler_params=pltpu.CompilerParams(
            dimension_semantics=("arbitrary",),
            vmem_limit_bytes=VMEM_LIMIT_BYTES),
        name="peer_route",
    )(x, gain, wq_t, kbig)


def _dense_kernel(x_ref, xnt_ref, q2h_ref, e2h_ref, thr_ref, e1_ref, u_ref, vt_ref,
                  y_ref, acc_ref, w_scr, *, tile, echunk):
    j = pl.program_id(1)

    @pl.when(j == 0)
    def _():
        acc_ref[...] = jnp.zeros_like(acc_ref)

    mxu_w = min(tile, MXU_DIM)
    ngrp = PEER_NKEYS // BF16_ROWS

    def bcast_row(word_ref, r, l0):
        row = word_ref[l0 // LANES, r:r + 1, :]
        return pltpu.bitcast(jnp.broadcast_to(row, (8, LANES)), BF16)

    h_all = jnp.dot(pltpu.bitcast(u_ref[0], BF16), pltpu.bitcast(xnt_ref[...], BF16),
                         preferred_element_type=F32)
    na = echunk // PEER_NKEYS
    ablk = 4
    gblk = 4
    for mb in range(tile // mxu_w):
        m0 = mb * mxu_w
        for lb in range(mxu_w // LANES):
            l0 = m0 + lb * LANES
            for gb in range(ngrp // gblk):
                grp = [gb * gblk + g for g in range(gblk)]
                for ab in range(na // ablk):
                    keys = [ab * ablk + a for a in range(ablk)]
                    gsum = [[jnp.zeros((BF16_ROWS, LANES), BF16) for _ in grp] for _ in keys]
                    for h in range(PEER_HEADS):
                        q2 = [pltpu.bitcast(q2h_ref[h, 8 * g:8 * g + 8, l0:l0 + LANES], BF16)
                              for g in grp]
                        e2 = [pltpu.bitcast(e2h_ref[h, 8 * g:8 * g + 8, l0:l0 + LANES], BF16)
                              for g in grp]
                        for ai, a in enumerate(keys):
                            thr = bcast_row(thr_ref, a * PEER_HEADS + h, l0)
                            e1 = bcast_row(e1_ref, a * PEER_HEADS + h, l0)
                            for g in range(gblk):
                                hit = q2[g] >= thr
                                gsum[ai][g] = gsum[ai][g] + jnp.where(hit, e2[g], 0.0) * e1
                    for ai, a in enumerate(keys):
                        for g in range(gblk):
                            e0 = a * PEER_NKEYS + grp[g] * BF16_ROWS
                            hv = h_all[e0:e0 + BF16_ROWS, l0:l0 + LANES]
                            act = (hv * (1.0 + lax.erf(hv * (2.0 ** -0.5)))).astype(BF16)
                            w_scr[e0:e0 + BF16_ROWS, l0:l0 + LANES] = (
                                jnp.where(gsum[ai][g] > 0.0, act, 0.0) * gsum[ai][g])
        acc_ref[:, m0:m0 + mxu_w] += jnp.dot(
            pltpu.bitcast(vt_ref[0], BF16), w_scr[:, m0:m0 + mxu_w],
            preferred_element_type=F32)

    @pl.when(j == pl.num_programs(1) - 1)
    def _():
        y_ref[...] = x_ref[...] + acc_ref[...].T


def _peer_dense(x, xnt, q2h, e2h, thr, e1, u_words, vt_words, *, layer, tile, echunk):
    n = x.shape[0]
    nt = n // tile
    nchunk = PEER_N_EXPERTS // echunk
    crows = (echunk // PEER_NKEYS) * PEER_HEADS
    kern = functools.partial(_dense_kernel, tile=tile, echunk=echunk)
    return pl.pallas_call(
        kern,
        grid=(nt, nchunk),
        in_specs=[
            pl.BlockSpec((tile, D_MODEL), lambda i, j: (i, 0)),
            pl.BlockSpec((D_MODEL // 2, tile), lambda i, j: (0, i)),
            pl.BlockSpec((PEER_HEADS, PEER_NKEYS // 2, tile), lambda i, j: (0, 0, i)),
            pl.BlockSpec((PEER_HEADS, PEER_NKEYS // 2, tile), lambda i, j: (0, 0, i)),
            pl.BlockSpec((tile // LANES, crows, LANES), lambda i, j: (i, j, 0)),
            pl.BlockSpec((tile // LANES, crows, LANES), lambda i, j: (i, j, 0)),
            pl.BlockSpec((1, echunk // 2, D_MODEL), lambda i, j: (layer, j, 0)),
            pl.BlockSpec((1, D_MODEL // 2, echunk), lambda i, j: (layer, 0, j)),
        ],
        out_specs=pl.BlockSpec((tile, D_MODEL), lambda i, j: (i, 0)),
        out_shape=jax.ShapeDtypeStruct((n, D_MODEL), F32),
        scratch_shapes=[pltpu.VMEM((D_MODEL, tile), F32),
                        pltpu.VMEM((echunk, tile), BF16)],
        compiler_params=pltpu.CompilerParams(
            dimension_semantics=("arbitrary", "arbitrary"),
            vmem_limit_bytes=VMEM_LIMIT_BYTES),
        name="peer_dense",
    )(x, xnt, q2h, e2h, thr, e1, u_words, vt_words)


def _peer(x, gain, wq_t, kbig, u_words, vt_words, *, layer, route_tile, dense_tile, echunk):
    xnt, q2h, e2h, thr, e1 = _peer_route(x, gain, wq_t, kbig, tile=route_tile)
    return _peer_dense(x, xnt, q2h, e2h, thr, e1, u_words, vt_words, layer=layer,
                       tile=dense_tile, echunk=echunk)


def _even_params(norm_g, w_in, conv_w, q_gain, k_gain, sinks, w_out):
    qcols = np.array([1536 + (j + 4 * hf) * HEAD_DIM + d
                      for j in range(4) for hf in range(2) for d in range(HEAD_DIM)])
    cols = np.concatenate([np.arange(1536), qcols, np.arange(2048, EVEN_IN_DIM)])
    orow = np.array([CONV_DIM + (j + 4 * hf) * HEAD_DIM + d
                     for j in range(4) for hf in range(2) for d in range(HEAD_DIM)])
    rows = np.concatenate([np.arange(CONV_DIM), orow])
    blk = lambda n: jnp.asarray(
        (np.arange(n)[:, None] // HEAD_DIM) == (np.arange(n)[None, :] // HEAD_DIM)).astype(BF16)
    return dict(
        gain=norm_g.reshape(1, D_MODEL),
        win=w_in[:, cols].astype(BF16),
        convw=conv_w,
        qg=jnp.tile(q_gain, N_Q_HEADS).reshape(1, ATTN_DIM),
        kg=jnp.tile(k_gain, N_KV_HEADS).reshape(1, KV_DIM),
        sink_rows=jnp.repeat(sinks, CHUNK).reshape(N_Q_HEADS * CHUNK, 1),
        hsum_q=blk(ATTN_DIM),
        hsum_k=blk(KV_DIM),
        wout=w_out[rows, :].astype(BF16),
    )


def _pack_tables_kernel(u_ref, v_ref, uo_ref, vo_ref):
    uo_ref[0] = pltpu.bitcast(u_ref[0].astype(BF16), jnp.int32)
    vo_ref[0] = pltpu.bitcast(v_ref[0].T.astype(BF16), jnp.int32)


def _pack_tables(u_tab, v_tab, *, eblk):
    nl, ne, d = u_tab.shape
    return pl.pallas_call(
        _pack_tables_kernel,
        grid=(nl, ne // eblk),
        in_specs=[pl.BlockSpec((1, eblk, d), lambda l, i: (l, i, 0)),
                  pl.BlockSpec((1, eblk, d), lambda l, i: (l, i, 0))],
        out_specs=[pl.BlockSpec((1, eblk // 2, d), lambda l, i: (l, i, 0)),
                   pl.BlockSpec((1, d // 2, eblk), lambda l, i: (l, 0, i))],
        out_shape=[jax.ShapeDtypeStruct((nl, ne // 2, d), jnp.int32),
                   jax.ShapeDtypeStruct((nl, d // 2, ne), jnp.int32)],
        compiler_params=pltpu.CompilerParams(
            dimension_semantics=("arbitrary", "arbitrary"),
            vmem_limit_bytes=VMEM_LIMIT_BYTES),
        name="pack_tables",
    )(u_tab, v_tab)


def _peer_params(norm_g, w_query, sub_keys):
    wq_t = w_query.T.reshape(PEER_HEADS, 2, 128, D_MODEL).transpose(1, 0, 2, 3)
    wq_t = wq_t.reshape(2 * PEER_HEADS * 128, D_MODEL).astype(BF16)
    eye = jnp.eye(PEER_HEADS, dtype=sub_keys.dtype)
    kbig = jnp.einsum('hpkd,hg->pkhgd', sub_keys, eye).reshape(
        2, PEER_NKEYS * PEER_HEADS, PEER_HEADS * 128)
    khm = jnp.einsum('hkd,hg->hkgd', sub_keys[:, 1], eye).reshape(
        1, PEER_HEADS * PEER_NKEYS, PEER_HEADS * 128)
    kbig = jnp.concatenate([kbig, khm], axis=0).astype(BF16)
    return dict(gain=norm_g.reshape(1, D_MODEL), wq_t=wq_t, kbig=kbig)


def kernel(x_prompt, x_sample, cache_conv, cache_k, cache_v, state_hgrn, norm_mix, norm_ffn,
           even_w_in, even_conv_w, even_q_gain, even_k_gain, even_sinks, even_w_out,
           hgrn_w_in, hgrn_lb, hgrn_out_gain, hgrn_w_out,
           peer_w_query, peer_sub_keys, peer_u, peer_v):
    bp, sp, _ = x_prompt.shape
    bs, ss, _ = x_sample.shape

    ev = _even_params(norm_mix[0], even_w_in[0], even_conv_w[0], even_q_gain[0],
                      even_k_gain[0], even_sinks[0], even_w_out[0])
    u_words, vt_words = _pack_tables(peer_u, peer_v, eblk=PEER_PACK_BLOCK)
    pe = [_peer_params(norm_ffn[l], peer_w_query[l], peer_sub_keys[l]) for l in range(2)]
    lbs = jax.nn.softmax(hgrn_lb.astype(F32), axis=0)
    lbs = jnp.cumsum(lbs, axis=0) - lbs[0]
    hg = dict(gain=norm_mix[1].reshape(1, D_MODEL), win=hgrn_w_in[0].astype(BF16),
              lb=lbs[1].reshape(1, D_MODEL), out_gain=hgrn_out_gain[0].reshape(1, D_MODEL),
              wout=hgrn_w_out[0].astype(BF16))

    def peer(x2d, l, route_tile, dense_tile):
        return _peer(x2d, pe[l]['gain'], pe[l]['wq_t'], pe[l]['kbig'], u_words, vt_words,
                     layer=l, route_tile=route_tile, dense_tile=dense_tile,
                     echunk=PEER_EXPERT_CHUNK)

    zc = jnp.zeros((bp, 2, CONV_DIM), F32)
    zkv = jnp.zeros((bp, WINDOW, KV_DIM), F32)
    rt = MIXER_ROW_TILE
    x, conv_p, k_p, v_p = _even_layer(x_prompt, zc, zkv, zkv, **ev,
                                      tile=rt, valid_rows=rt, has_cache=False)
    x = peer(x.reshape(bp * sp, D_MODEL), 0, PEER_ROUTE_TILE, PEER_DENSE_TILE)
    s0 = jnp.zeros((bp, HGRN_HEADS, HGRN_DK, HGRN_DK), F32)
    x, s_p = _hgrn_layer(x.reshape(bp, sp, D_MODEL), s0, **hg, tile=rt, valid_rows=rt)
    y_prompt = peer(x.reshape(bp * sp, D_MODEL), 1, PEER_ROUTE_TILE, PEER_DENSE_TILE)
    y_prompt = y_prompt.reshape(bp, sp, D_MODEL)

    ns = bs * ss
    xs = jnp.pad(x_sample, ((0, 0), (0, CHUNK - ss), (0, 0)))
    xs, conv_s, k_s, v_s = _even_layer(
        xs, cache_conv[0], cache_k[0].reshape(bs, WINDOW, KV_DIM),
        cache_v[0].reshape(bs, WINDOW, KV_DIM), **ev, tile=CHUNK, valid_rows=ss, has_cache=True)
    xs = peer(xs[:, :ss].reshape(ns, D_MODEL), 0, ns, ns).reshape(bs, ss, D_MODEL)
    xs = jnp.pad(xs, ((0, 0), (0, HGRN_CHUNK - ss), (0, 0)))
    xs, s_s = _hgrn_layer(xs, state_hgrn[0], **hg, tile=HGRN_CHUNK, valid_rows=ss)
    y_sample = peer(xs[:, :ss].reshape(ns, D_MODEL), 1, ns, ns).reshape(bs, ss, D_MODEL)

    kv5 = lambda a, b: a.reshape(1, b, WINDOW, N_KV_HEADS, HEAD_DIM)
    return (y_prompt, y_sample, conv_p[None], kv5(k_p, bp), kv5(v_p, bp), s_p[None],
            conv_s[None], kv5(k_s, bs), kv5(v_s, bs), s_s[None])
```

```python
import functools

import jax
import jax.numpy as jnp
import numpy as np
from jax import lax
from jax.experimental import pallas as pl
from jax.experimental.pallas import tpu as pltpu

F32 = jnp.float32
BF16 = jnp.bfloat16

D_MODEL = 1024
RMS_EPS = 1e-6
CHUNK = 64
WINDOW = 128
CONV_DIM = 512
N_Q_HEADS = 8
N_KV_HEADS = 2
HEAD_DIM = 64
ATTN_DIM = 512
KV_DIM = 128
EVEN_IN_DIM = 2304
HGRN_HEADS = 8
HGRN_DK = 128
HGRN_BLOCK = 16
HGRN_CHUNK = 128
PEER_HEADS = 8
PEER_NKEYS = 128
PEER_TOPK = 16
PEER_N_EXPERTS = PEER_NKEYS * PEER_NKEYS
LANES = 128
BF16_ROWS = 16
MXU_DIM = 256

MIXER_ROW_TILE = 256
PEER_ROUTE_TILE = 256
PEER_DENSE_TILE = 1024
PEER_EXPERT_CHUNK = 1024
PEER_PACK_BLOCK = 1024

VMEM_LIMIT_BYTES = 52 * 1024 * 1024

NEG_INF = float("-inf")


def _rms_scale(x):
    return lax.rsqrt(jnp.mean(x * x, axis=-1, keepdims=True) + RMS_EPS)


def _split_dot(x, w_bf16):
    hi = x.astype(BF16)
    lo = (x - hi.astype(F32)).astype(BF16)
    return (jnp.dot(hi, w_bf16, preferred_element_type=F32)
            + jnp.dot(lo, w_bf16, preferred_element_type=F32))


def _even_kernel(x_ref, conv0_ref, kc0_ref, vc0_ref, g_ref, win_ref, convw_ref,
                 qg_ref, kg_ref, sink_ref, hsum_q_ref, hsum_k_ref, wout_ref,
                 y_ref, nconv_ref, nk_ref, nv_ref,
                 u_scr, k_scr, v_scr, mix_scr,
                 *, tile, valid_rows, has_cache):
    t = pl.program_id(1)

    @pl.when(t == 0)
    def _():
        u_scr[0:8, :] = jnp.zeros((8, CONV_DIM), F32)
        u_scr[6:8, :] = conv0_ref[0]
        k_scr[0:WINDOW, :] = kc0_ref[0]
        v_scr[0:WINDOW, :] = vc0_ref[0]

    x = x_ref[0]
    xn = x * _rms_scale(x) * g_ref[...]
    z = jnp.dot(xn.astype(BF16), win_ref[...], preferred_element_type=F32)
    bg = z[:, 0:512]
    cg = z[:, 512:1024]
    hh = z[:, 1024:1536]
    q = z[:, 1536:2048]
    k = z[:, 2048:2176]
    v = z[:, 2176:2304]

    u = cg * hh
    u_scr[8:8 + tile, :] = u
    cw = convw_ref[...]
    conv = (cw[0:1, :] * u_scr[6:6 + tile, :] + cw[1:2, :] * u_scr[7:7 + tile, :]
            + cw[2:3, :] * u)
    mix_scr[:, 0:CONV_DIM] = bg * conv
    tail = u_scr[6 + valid_rows:8 + valid_rows, :]
    nconv_ref[0] = tail
    u_scr[6:8, :] = tail

    q_ms = _split_dot(q * q, hsum_q_ref[...]) * (1.0 / HEAD_DIM)
    q = q * lax.rsqrt(q_ms + RMS_EPS) * qg_ref[...] * (HEAD_DIM ** -0.5)
    k_ms = _split_dot(k * k, hsum_k_ref[...]) * (1.0 / HEAD_DIM)
    k = k * lax.rsqrt(k_ms + RMS_EPS) * kg_ref[...]
    k_scr[WINDOW:WINDOW + tile, :] = k
    v_scr[WINDOW:WINDOW + tile, :] = v

    lane = lax.broadcasted_iota(jnp.int32, (CHUNK, KV_DIM), 1)
    low_half = lane < HEAD_DIM
    sink = sink_ref[...]
    nkeys = WINDOW + CHUNK
    col = lax.broadcasted_iota(jnp.int32, (N_Q_HEADS * CHUNK, nkeys), 1)
    for j in range(tile // CHUNK):
        r0 = j * CHUNK
        blocks = []
        for b in range(N_Q_HEADS):
            qv = q[r0:r0 + CHUNK, (b % 4) * KV_DIM:(b % 4 + 1) * KV_DIM]
            keep = low_half if b < 4 else jnp.logical_not(low_half)
            blocks.append(jnp.where(keep, qv, 0.0))
        qs = jnp.concatenate(blocks, axis=0).astype(BF16)
        kw = k_scr[r0:r0 + nkeys, :].astype(BF16)
        vw = v_scr[r0:r0 + nkeys, :].astype(BF16)
        s = lax.dot_general(qs, kw, (((1,), (1,)), ((), ())),
                            preferred_element_type=F32)
        if valid_rows < tile:
            s = jnp.where(col < WINDOW + valid_rows, s, NEG_INF)
        if not has_cache and r0 < WINDOW:
            s = jnp.where(jnp.logical_or(col >= WINDOW - r0, t > 0), s, NEG_INF)
        m = jnp.maximum(jnp.max(s, axis=-1, keepdims=True), sink)
        p = jnp.exp(s - m)
        p = p / (jnp.sum(p, axis=-1, keepdims=True) + jnp.exp(sink - m))
        o = jnp.dot(p.astype(BF16), vw, preferred_element_type=F32)
        for jj in range(4):
            oj = jnp.where(low_half, o[jj * CHUNK:(jj + 1) * CHUNK, :],
                           o[(4 + jj) * CHUNK:(5 + jj) * CHUNK, :])
            mix_scr[r0:r0 + CHUNK, CONV_DIM + jj * KV_DIM:CONV_DIM + (jj + 1) * KV_DIM] = oj

    nk = k_scr[valid_rows:valid_rows + WINDOW, :]
    nv = v_scr[valid_rows:valid_rows + WINDOW, :]
    nk_ref[0] = nk
    nv_ref[0] = nv
    k_scr[0:WINDOW, :] = nk
    v_scr[0:WINDOW, :] = nv

    mix = jnp.dot(mix_scr[...].astype(BF16), wout_ref[...], preferred_element_type=F32)
    y_ref[0] = x + mix


def _even_layer(x, conv0, kc0, vc0, gain, win, convw, qg, kg, sink_rows, hsum_q, hsum_k, wout,
                *, tile, valid_rows, has_cache):
    b, s, _ = x.shape
    nt = s // tile
    const2 = lambda i, j: (0, 0)
    per_b = lambda i, j: (i, 0, 0)
    kern = functools.partial(_even_kernel, tile=tile, valid_rows=valid_rows, has_cache=has_cache)
    return pl.pallas_call(
        kern,
        grid=(b, nt),
        in_specs=[
            pl.BlockSpec((1, tile, D_MODEL), lambda i, j: (i, j, 0)),
            pl.BlockSpec((1, 2, CONV_DIM), per_b),
            pl.BlockSpec((1, WINDOW, KV_DIM), per_b),
            pl.BlockSpec((1, WINDOW, KV_DIM), per_b),
            pl.BlockSpec((1, D_MODEL), const2),
            pl.BlockSpec((D_MODEL, EVEN_IN_DIM), const2),
            pl.BlockSpec((3, CONV_DIM), const2),
            pl.BlockSpec((1, ATTN_DIM), const2),
            pl.BlockSpec((1, KV_DIM), const2),
            pl.BlockSpec((N_Q_HEADS * CHUNK, 1), const2),
            pl.BlockSpec((ATTN_DIM, ATTN_DIM), const2),
            pl.BlockSpec((KV_DIM, KV_DIM), const2),
            pl.BlockSpec((D_MODEL, D_MODEL), const2),
        ],
        out_specs=[
            pl.BlockSpec((1, tile, D_MODEL), lambda i, j: (i, j, 0)),
            pl.BlockSpec((1, 2, CONV_DIM), per_b),
            pl.BlockSpec((1, WINDOW, KV_DIM), per_b),
            pl.BlockSpec((1, WINDOW, KV_DIM), per_b),
        ],
        out_shape=[
            jax.ShapeDtypeStruct((b, s, D_MODEL), F32),
            jax.ShapeDtypeStruct((b, 2, CONV_DIM), F32),
            jax.ShapeDtypeStruct((b, WINDOW, KV_DIM), F32),
            jax.ShapeDtypeStruct((b, WINDOW, KV_DIM), F32),
        ],
        scratch_shapes=[
            pltpu.VMEM((8 + tile, CONV_DIM), F32),
            pltpu.VMEM((WINDOW + tile, KV_DIM), F32),
            pltpu.VMEM((WINDOW + tile, KV_DIM), F32),
            pltpu.VMEM((tile, D_MODEL), F32),
        ],
        compiler_params=pltpu.CompilerParams(
            dimension_semantics=("arbitrary", "arbitrary"),
            vmem_limit_bytes=VMEM_LIMIT_BYTES),
        name="even_layer",
    )(x, conv0, kc0, vc0, gain, win, convw, qg, kg, sink_rows, hsum_q, hsum_k, wout)


def _hgrn_kernel(x_ref, s0_ref, g_ref, win_ref, lb_ref, og_ref, wout_ref,
                 y_ref, snew_ref,
                 s_scr, q_scr, k_scr, v_scr, lf_scr, o_scr,
                 *, tile, valid_rows):
    t = pl.program_id(1)

    @pl.when(t == 0)
    def _():
        s_scr[...] = s0_ref[0]

    x = x_ref[0]
    xn = x * _rms_scale(x) * g_ref[...]
    z = jnp.dot(xn.astype(BF16), win_ref[...], preferred_element_type=F32)
    lb = lb_ref[...]
    fg = lb + (1.0 - lb) * jax.nn.sigmoid(z[:, D_MODEL:2 * D_MODEL])
    logf = jnp.log(fg)
    kk = 1.0 - fg
    if valid_rows < tile:
        row = lax.broadcasted_iota(jnp.int32, (tile, D_MODEL), 0)
        live = row < valid_rows
        logf = jnp.where(live, logf, 0.0)
        kk = jnp.where(live, kk, 0.0)
    q_scr[...] = z[:, 0:D_MODEL]
    k_scr[...] = kk
    v_scr[...] = z[:, 2 * D_MODEL:3 * D_MODEL]
    gate = z[:, 3 * D_MODEL:4 * D_MODEL]

    ri = lax.broadcasted_iota(jnp.int32, (tile, tile), 0)
    ci = lax.broadcasted_iota(jnp.int32, (tile, tile), 1)
    same = (ri // HGRN_CHUNK) == (ci // HGRN_CHUNK)
    tril = jnp.where(jnp.logical_and(same, ci <= ri), 1.0, 0.0).astype(BF16)
    hi = logf.astype(BF16)
    lo = (logf - hi.astype(F32)).astype(BF16)
    lf_scr[...] = (jnp.dot(tril, hi, preferred_element_type=F32)
                   + jnp.dot(tril, lo, preferred_element_type=F32))

    for c in range(tile // HGRN_CHUNK):
        c0 = c * HGRN_CHUNK
        g = lf_scr[c0:c0 + HGRN_CHUNK, :]
        gtot = lf_scr[c0 + HGRN_CHUNK - 1:c0 + HGRN_CHUNK, :]
        qe = (q_scr[c0:c0 + HGRN_CHUNK, :] * jnp.exp(g)).astype(BF16)
        kh = k_scr[c0:c0 + HGRN_CHUNK, :] * jnp.exp(gtot - g)
        kh_t = kh.T.astype(BF16)
        dec_t = jnp.broadcast_to(jnp.exp(gtot), (HGRN_CHUNK, D_MODEL)).T
        vc = v_scr[c0:c0 + HGRN_CHUNK, :].astype(BF16)
        for h in range(HGRN_HEADS):
            hs = slice(h * HGRN_DK, (h + 1) * HGRN_DK)
            s_h = s_scr[h]
            o_scr[c0:c0 + HGRN_CHUNK, hs] = jnp.dot(
                qe[:, hs], s_h.astype(BF16), preferred_element_type=F32)
            s_scr[h] = dec_t[hs, :] * s_h + jnp.dot(
                kh_t[hs, :], vc[:, hs], preferred_element_type=F32)
        for j in range(HGRN_CHUNK // HGRN_BLOCK):
            r0 = c0 + j * HGRN_BLOCK
            nrow = HGRN_CHUNK - j * HGRN_BLOCK
            gj = lf_scr[r0:c0 + HGRN_CHUNK, :]
            if j == 0:
                rel = gj
            else:
                rel = gj - lf_scr[r0 - 1:r0, :]
            qj = (q_scr[r0:c0 + HGRN_CHUNK, :] * jnp.exp(rel)).astype(BF16)
            kj = (k_scr[r0:r0 + HGRN_BLOCK, :] * jnp.exp(-rel[0:HGRN_BLOCK, :])).astype(BF16)
            vj = v_scr[r0:r0 + HGRN_BLOCK, :].astype(BF16)
            causal = (lax.broadcasted_iota(jnp.int32, (nrow, HGRN_BLOCK), 0)
                      >= lax.broadcasted_iota(jnp.int32, (nrow, HGRN_BLOCK), 1))
            for h in range(HGRN_HEADS):
                hs = slice(h * HGRN_DK, (h + 1) * HGRN_DK)
                a = lax.dot_general(qj[:, hs], kj[:, hs], (((1,), (1,)), ((), ())),
                                    preferred_element_type=F32)
                a = jnp.where(causal, a, 0.0).astype(BF16)
                o_scr[r0:c0 + HGRN_CHUNK, hs] += jnp.dot(
                    a, vj[:, hs], preferred_element_type=F32)

    snew_ref[0] = s_scr[...]
    o = o_scr[...]
    o = o * _rms_scale(o) * og_ref[...]
    o = o * (gate * jax.nn.sigmoid(gate))
    y_ref[0] = x + jnp.dot(o.astype(BF16), wout_ref[...], preferred_element_type=F32)


def _hgrn_layer(x, s0, gain, win, lb, out_gain, wout, *, tile, valid_rows):
    b, s, _ = x.shape
    nt = s // tile
    const2 = lambda i, j: (0, 0)
    kern = functools.partial(_hgrn_kernel, tile=tile, valid_rows=valid_rows)
    return pl.pallas_call(
        kern,
        grid=(b, nt),
        in_specs=[
            pl.BlockSpec((1, tile, D_MODEL), lambda i, j: (i, j, 0)),
            pl.BlockSpec((1, HGRN_HEADS, HGRN_DK, HGRN_DK), lambda i, j: (i, 0, 0, 0)),
            pl.BlockSpec((1, D_MODEL), const2),
            pl.BlockSpec((D_MODEL, 4 * D_MODEL), const2),
            pl.BlockSpec((1, D_MODEL), const2),
            pl.BlockSpec((1, D_MODEL), const2),
            pl.BlockSpec((D_MODEL, D_MODEL), const2),
        ],
        out_specs=[
            pl.BlockSpec((1, tile, D_MODEL), lambda i, j: (i, j, 0)),
            pl.BlockSpec((1, HGRN_HEADS, HGRN_DK, HGRN_DK), lambda i, j: (i, 0, 0, 0)),
        ],
        out_shape=[
            jax.ShapeDtypeStruct((b, s, D_MODEL), F32),
            jax.ShapeDtypeStruct((b, HGRN_HEADS, HGRN_DK, HGRN_DK), F32),
        ],
        scratch_shapes=[
            pltpu.VMEM((HGRN_HEADS, HGRN_DK, HGRN_DK), F32),
            pltpu.VMEM((tile, D_MODEL), F32),
            pltpu.VMEM((tile, D_MODEL), F32),
            pltpu.VMEM((tile, D_MODEL), F32),
            pltpu.VMEM((tile, D_MODEL), F32),
            pltpu.VMEM((tile, D_MODEL), F32),
        ],
        compiler_params=pltpu.CompilerParams(
            dimension_semantics=("arbitrary", "arbitrary"),
            vmem_limit_bytes=VMEM_LIMIT_BYTES),
        name="hgrn_layer",
    )(x, s0, gain, win, lb, out_gain, wout)


def _sort_network(n):
    pairs = []

    def merge(lo, hi, r):
        step = r * 2
        if step < hi - lo:
            merge(lo, hi, step)
            merge(lo + r, hi, step)
            pairs.extend((i, i + r) for i in range(lo + r, hi - r, step))
        else:
            pairs.append((lo, lo + r))

    def sort(lo, hi):
        if hi - lo >= 1:
            mid = lo + (hi - lo) // 2
            sort(lo, mid)
            sort(mid + 1, hi)
            merge(lo, hi, 1)

    sort(0, n - 1)
    return pairs


_SORT16 = _sort_network(PEER_TOPK)


def _sort16_desc(v):
    v = list(v)
    for i, j in _SORT16:
        v[i], v[j] = jnp.maximum(v[i], v[j]), jnp.minimum(v[i], v[j])
    return v


def _merge_top16(a, b):
    c = [jnp.maximum(a[i], b[PEER_TOPK - 1 - i]) for i in range(PEER_TOPK)]
    for d in (8, 4, 2, 1):
        for i in range(PEER_TOPK):
            if not i & d:
                c[i], c[i + d] = jnp.maximum(c[i], c[i + d]), jnp.minimum(c[i], c[i + d])
    return c


def _top16_sorted(load_slab):
    lists = [_sort16_desc([load_slab(PEER_TOPK * g + i) for i in range(PEER_TOPK)])
             for g in range(PEER_NKEYS // PEER_TOPK)]
    while len(lists) > 1:
        lists = [_merge_top16(a, b) for a, b in zip(lists[0::2], lists[1::2])]
    return lists[0]


def _bf16_pair_words(x):
    bits = lax.bitcast_convert_type(x.astype(BF16).astype(F32), jnp.int32)
    return bits | lax.shift_right_logical(bits, 16)


def _route_kernel(x_ref, g_ref, wq_ref, kb_ref,
                  xnt_ref, q2h_ref, e2h_ref, thr_ref, e1_ref,
                  s1_scr, s2_scr, a_scr, b_scr, cand_scr,
                  *, tile):
    x = x_ref[...]
    xn = x * _rms_scale(x) * g_ref[...]
    xnt = xn.T.astype(BF16)
    xnt_ref[...] = pltpu.bitcast(xnt, jnp.int32)
    qt = jnp.dot(wq_ref[...], xnt, preferred_element_type=F32).astype(BF16)
    half = PEER_HEADS * 128
    s1 = jnp.dot(kb_ref[0], qt[0:half, :], preferred_element_type=F32)
    s2 = jnp.dot(kb_ref[1], qt[half:2 * half, :], preferred_element_type=F32)
    s1_scr[...] = s1.reshape(PEER_NKEYS, PEER_HEADS, tile)
    s2_scr[...] = s2.reshape(PEER_NKEYS, PEER_HEADS, tile)
    for lt in range(tile // LANES):
        ls = slice(lt * LANES, (lt + 1) * LANES)
        for src, dst in ((s1_scr, a_scr), (s2_scr, b_scr)):
            top = _top16_sorted(lambda kidx, src=src: src[kidx, :, ls])
            for i in range(PEER_TOPK):
                dst[i, :, ls] = top[i]

    pairs = [(i, j) for i in range(PEER_TOPK) for j in range(PEER_TOPK)
             if (i + 1) * (j + 1) <= PEER_TOPK]
    rows = [[] for _ in range(PEER_TOPK)]
    for n, (i, j) in enumerate(pairs):
        c = a_scr[i] + b_scr[j]
        cand_scr[n] = c
        rows[i].append(c)
    s_a = _sort16_desc(rows[1] + [rows[i][0] for i in range(8, PEER_TOPK)])
    s_b = _sort16_desc(rows[2] + rows[3] + rows[4] + rows[5] + rows[6])
    top = _merge_top16(_merge_top16(rows[0], s_a), s_b)
    tau = jnp.minimum(top[13], jnp.minimum(jnp.maximum(top[14], rows[7][1]),
                                           jnp.maximum(top[15], rows[7][0])))

    a0 = a_scr[0]
    b0 = b_scr[0]
    zsum = jnp.zeros((PEER_HEADS, tile), F32)
    codes = []
    for i in range(PEER_TOPK):
        lam = jnp.zeros((PEER_HEADS, tile), F32)
        ea = jnp.exp(a_scr[i] - a0)
        for n, (pi, pj) in enumerate(pairs):
            if pi != i:
                continue
            sel = cand_scr[n] >= tau
            lam = lam + jnp.where(sel, 1.0, 0.0)
            zsum = zsum + jnp.where(sel, ea * jnp.exp(b_scr[pj] - b0), 0.0)
        codes.append((PEER_TOPK + 1.0) - lam)

    s1 = s1_scr[...]
    thr = jnp.full((PEER_NKEYS, PEER_HEADS, tile), PEER_TOPK + 1.0, F32)
    for i in range(PEER_TOPK):
        thr = jnp.where(s1 == a_scr[i][None], codes[i][None], thr)
    thr_w = _bf16_pair_words(thr).reshape(PEER_NKEYS * PEER_HEADS, tile)
    e1_w = _bf16_pair_words(jnp.exp(s1 - a0[None])).reshape(PEER_NKEYS * PEER_HEADS, tile)
    for lt in range(tile // LANES):
        thr_ref[lt] = thr_w[:, lt * LANES:(lt + 1) * LANES]
        e1_ref[lt] = e1_w[:, lt * LANES:(lt + 1) * LANES]

    s2hm = jnp.dot(kb_ref[2], qt[half:2 * half, :], preferred_element_type=F32)
    zscale = 0.5 / zsum
    for h in range(PEER_HEADS):
        s2h = s2hm[h * PEER_NKEYS:(h + 1) * PEER_NKEYS, :]
        q2 = jnp.zeros((PEER_NKEYS, tile), F32)
        for jj in range(PEER_TOPK - 1, -1, -1):
            q2 = jnp.where(s2h >= b_scr[jj, h:h + 1, :], float(PEER_TOPK - jj), q2)
        q2h_ref[h] = pltpu.bitcast(q2.astype(BF16), jnp.int32)
        e2 = jnp.where(q2 > 0.0, jnp.exp(s2h - b0[h:h + 1, :]) * zscale[h:h + 1, :], 0.0)
        e2h_ref[h] = pltpu.bitcast(e2.astype(BF16), jnp.int32)


def _peer_route(x, gain, wq_t, kbig, *, tile):
    n = x.shape[0]
    nt = n // tile
    npairs = sum(1 for i in range(PEER_TOPK) for j in range(PEER_TOPK)
                 if (i + 1) * (j + 1) <= PEER_TOPK)
    rows = PEER_NKEYS * PEER_HEADS
    kern = functools.partial(_route_kernel, tile=tile)
    return pl.pallas_call(
        kern,
        grid=(nt,),
        in_specs=[
            pl.BlockSpec((tile, D_MODEL), lambda i: (i, 0)),
            pl.BlockSpec((1, D_MODEL), lambda i: (0, 0)),
            pl.BlockSpec((2 * rows, D_MODEL), lambda i: (0, 0)),
            pl.BlockSpec((3, rows, rows), lambda i: (0, 0, 0)),
        ],
        out_specs=[
            pl.BlockSpec((D_MODEL // 2, tile), lambda i: (0, i)),
            pl.BlockSpec((PEER_HEADS, PEER_NKEYS // 2, tile), lambda i: (0, 0, i)),
            pl.BlockSpec((PEER_HEADS, PEER_NKEYS // 2, tile), lambda i: (0, 0, i)),
            pl.BlockSpec((tile // LANES, rows, LANES), lambda i: (i, 0, 0)),
            pl.BlockSpec((tile // LANES, rows, LANES), lambda i: (i, 0, 0)),
        ],
        out_shape=[
            jax.ShapeDtypeStruct((D_MODEL // 2, n), jnp.int32),
            jax.ShapeDtypeStruct((PEER_HEADS, PEER_NKEYS // 2, n), jnp.int32),
            jax.ShapeDtypeStruct((PEER_HEADS, PEER_NKEYS // 2, n), jnp.int32),
            jax.ShapeDtypeStruct((n // LANES, rows, LANES), jnp.int32),
            jax.ShapeDtypeStruct((n // LANES, rows, LANES), jnp.int32),
        ],
        scratch_shapes=[
            pltpu.VMEM((PEER_NKEYS, PEER_HEADS, tile), F32),
            pltpu.VMEM((PEER_NKEYS, PEER_HEADS, tile), F32),
            pltpu.VMEM((PEER_TOPK, PEER_HEADS, tile), F32),
            pltpu.VMEM((PEER_TOPK, PEER_HEADS, tile), F32),
            pltpu.VMEM((npairs, PEER_HEADS, tile), F32),
        ],
        compiler_params=pltpu.CompilerParams(
            dimension_semantics=("arbitrary",),
            vmem_limit_bytes=VMEM_LIMIT_BYTES),
        name="peer_route",
    )(x, gain, wq_t, kbig)


def _dense_kernel(x_ref, xnt_ref, q2h_ref, e2h_ref, thr_ref, e1_ref, u_ref, vt_ref,
                  y_ref, acc_ref, w_scr, *, tile, echunk):
    j = pl.program_id(1)

    @pl.when(j == 0)
    def _():
        acc_ref[...] = jnp.zeros_like(acc_ref)

    mxu_w = min(tile, MXU_DIM)
    ngrp = PEER_NKEYS // BF16_ROWS

    def bcast_row(word_ref, r, l0):
        row = word_ref[l0 // LANES, r:r + 1, :]
        return pltpu.bitcast(jnp.broadcast_to(row, (8, LANES)), BF16)

    h_all = jnp.dot(pltpu.bitcast(u_ref[0], BF16), pltpu.bitcast(xnt_ref[...], BF16),
                         preferred_element_type=F32)
    na = echunk // PEER_NKEYS
    ablk = 4
    gblk = 4
    for mb in range(tile // mxu_w):
        m0 = mb * mxu_w
        for lb in range(mxu_w // LANES):
            l0 = m0 + lb * LANES
            for gb in range(ngrp // gblk):
                grp = [gb * gblk + g for g in range(gblk)]
                for ab in range(na // ablk):
                    keys = [ab * ablk + a for a in range(ablk)]
                    gsum = [[jnp.zeros((BF16_ROWS, LANES), BF16) for _ in grp] for _ in keys]
                    for h in range(PEER_HEADS):
                        q2 = [pltpu.bitcast(q2h_ref[h, 8 * g:8 * g + 8, l0:l0 + LANES], BF16)
                              for g in grp]
                        e2 = [pltpu.bitcast(e2h_ref[h, 8 * g:8 * g + 8, l0:l0 + LANES], BF16)
                              for g in grp]
                        for ai, a in enumerate(keys):
                            thr = bcast_row(thr_ref, a * PEER_HEADS + h, l0)
                            e1 = bcast_row(e1_ref, a * PEER_HEADS + h, l0)
                            for g in range(gblk):
                                hit = q2[g] >= thr
                                gsum[ai][g] = gsum[ai][g] + jnp.where(hit, e2[g], 0.0) * e1
                    for ai, a in enumerate(keys):
                        for g in range(gblk):
                            e0 = a * PEER_NKEYS + grp[g] * BF16_ROWS
                            hv = h_all[e0:e0 + BF16_ROWS, l0:l0 + LANES]
                            act = (hv * (1.0 + lax.erf(hv * (2.0 ** -0.5)))).astype(BF16)
                            w_scr[e0:e0 + BF16_ROWS, l0:l0 + LANES] = (
                                jnp.where(gsum[ai][g] > 0.0, act, 0.0) * gsum[ai][g])
        acc_ref[:, m0:m0 + mxu_w] += jnp.dot(
            pltpu.bitcast(vt_ref[0], BF16), w_scr[:, m0:m0 + mxu_w],
            preferred_element_type=F32)

    @pl.when(j == pl.num_programs(1) - 1)
    def _():
        y_ref[...] = x_ref[...] + acc_ref[...].T


def _peer_dense(x, xnt, q2h, e2h, thr, e1, u_words, vt_words, *, layer, tile, echunk):
    n = x.shape[0]
    nt = n // tile
    nchunk = PEER_N_EXPERTS // echunk
    crows = (echunk // PEER_NKEYS) * PEER_HEADS
    kern = functools.partial(_dense_kernel, tile=tile, echunk=echunk)
    return pl.pallas_call(
        kern,
        grid=(nt, nchunk),
        in_specs=[
            pl.BlockSpec((tile, D_MODEL), lambda i, j: (i, 0)),
            pl.BlockSpec((D_MODEL // 2, tile), lambda i, j: (0, i)),
            pl.BlockSpec((PEER_HEADS, PEER_NKEYS // 2, tile), lambda i, j: (0, 0, i)),
            pl.BlockSpec((PEER_HEADS, PEER_NKEYS // 2, tile), lambda i, j: (0, 0, i)),
            pl.BlockSpec((tile // LANES, crows, LANES), lambda i, j: (i, j, 0)),
            pl.BlockSpec((tile // LANES, crows, LANES), lambda i, j: (i, j, 0)),
            pl.BlockSpec((1, echunk // 2, D_MODEL), lambda i, j: (layer, j, 0)),
            pl.BlockSpec((1, D_MODEL // 2, echunk), lambda i, j: (layer, 0, j)),
        ],
        out_specs=pl.BlockSpec((tile, D_MODEL), lambda i, j: (i, 0)),
        out_shape=jax.ShapeDtypeStruct((n, D_MODEL), F32),
        scratch_shapes=[pltpu.VMEM((D_MODEL, tile), F32),
                        pltpu.VMEM((echunk, tile), BF16)],
        compiler_params=pltpu.CompilerParams(
            dimension_semantics=("arbitrary", "arbitrary"),
            vmem_limit_bytes=VMEM_LIMIT_BYTES),
        name="peer_dense",
    )(x, xnt, q2h, e2h, thr, e1, u_words, vt_words)


def _peer(x, gain, wq_t, kbig, u_words, vt_words, *, layer, route_tile, dense_tile, echunk):
    xnt, q2h, e2h, thr, e1 = _peer_route(x, gain, wq_t, kbig, tile=route_tile)
    return _peer_dense(x, xnt, q2h, e2h, thr, e1, u_words, vt_words, layer=layer,
                       tile=dense_tile, echunk=echunk)


def _even_params(norm_g, w_in, conv_w, q_gain, k_gain, sinks, w_out):
    qcols = np.array([1536 + (j + 4 * hf) * HEAD_DIM + d
                      for j in range(4) for hf in range(2) for d in range(HEAD_DIM)])
    cols = np.concatenate([np.arange(1536), qcols, np.arange(2048, EVEN_IN_DIM)])
    orow = np.array([CONV_DIM + (j + 4 * hf) * HEAD_DIM + d
                     for j in range(4) for hf in range(2) for d in range(HEAD_DIM)])
    rows = np.concatenate([np.arange(CONV_DIM), orow])
    blk = lambda n: jnp.asarray(
        (np.arange(n)[:, None] // HEAD_DIM) == (np.arange(n)[None, :] // HEAD_DIM)).astype(BF16)
    return dict(
        gain=norm_g.reshape(1, D_MODEL),
        win=w_in[:, cols].astype(BF16),
        convw=conv_w,
        qg=jnp.tile(q_gain, N_Q_HEADS).reshape(1, ATTN_DIM),
        kg=jnp.tile(k_gain, N_KV_HEADS).reshape(1, KV_DIM),
        sink_rows=jnp.repeat(sinks, CHUNK).reshape(N_Q_HEADS * CHUNK, 1),
        hsum_q=blk(ATTN_DIM),
        hsum_k=blk(KV_DIM),
        wout=w_out[rows, :].astype(BF16),
    )


def _pack_tables_kernel(u_ref, v_ref, uo_ref, vo_ref):
    uo_ref[0] = pltpu.bitcast(u_ref[0].astype(BF16), jnp.int32)
    vo_ref[0] = pltpu.bitcast(v_ref[0].T.astype(BF16), jnp.int32)


def _pack_tables(u_tab, v_tab, *, eblk):
    nl, ne, d = u_tab.shape
    return pl.pallas_call(
        _pack_tables_kernel,
        grid=(nl, ne // eblk),
        in_specs=[pl.BlockSpec((1, eblk, d), lambda l, i: (l, i, 0)),
                  pl.BlockSpec((1, eblk, d), lambda l, i: (l, i, 0))],
        out_specs=[pl.BlockSpec((1, eblk // 2, d), lambda l, i: (l, i, 0)),
                   pl.BlockSpec((1, d // 2, eblk), lambda l, i: (l, 0, i))],
        out_shape=[jax.ShapeDtypeStruct((nl, ne // 2, d), jnp.int32),
                   jax.ShapeDtypeStruct((nl, d // 2, ne), jnp.int32)],
        compiler_params=pltpu.CompilerParams(
            dimension_semantics=("arbitrary", "arbitrary"),
            vmem_limit_bytes=VMEM_LIMIT_BYTES),
        name="pack_tables",
    )(u_tab, v_tab)


def _peer_params(norm_g, w_query, sub_keys):
    wq_t = w_query.T.reshape(PEER_HEADS, 2, 128, D_MODEL).transpose(1, 0, 2, 3)
    wq_t = wq_t.reshape(2 * PEER_HEADS * 128, D_MODEL).astype(BF16)
    eye = jnp.eye(PEER_HEADS, dtype=sub_keys.dtype)
    kbig = jnp.einsum('hpkd,hg->pkhgd', sub_keys, eye).reshape(
        2, PEER_NKEYS * PEER_HEADS, PEER_HEADS * 128)
    khm = jnp.einsum('hkd,hg->hkgd', sub_keys[:, 1], eye).reshape(
        1, PEER_HEADS * PEER_NKEYS, PEER_HEADS * 128)
    kbig = jnp.concatenate([kbig, khm], axis=0).astype(BF16)
    return dict(gain=norm_g.reshape(1, D_MODEL), wq_t=wq_t, kbig=kbig)


def kernel(x_prompt, x_sample, cache_conv, cache_k, cache_v, state_hgrn, norm_mix, norm_ffn,
           even_w_in, even_conv_w, even_q_gain, even_k_gain, even_sinks, even_w_out,
           hgrn_w_in, hgrn_lb, hgrn_out_gain, hgrn_w_out,
           peer_w_query, peer_sub_keys, peer_u, peer_v):
    bp, sp, _ = x_prompt.shape
    bs, ss, _ = x_sample.shape

    ev = _even_params(norm_mix[0], even_w_in[0], even_conv_w[0], even_q_gain[0],
                      even_k_gain[0], even_sinks[0], even_w_out[0])
    u_words, vt_words = _pack_tables(peer_u, peer_v, eblk=PEER_PACK_BLOCK)
    pe = [_peer_params(norm_ffn[l], peer_w_query[l], peer_sub_keys[l]) for l in range(2)]
    lbs = jax.nn.softmax(hgrn_lb.astype(F32), axis=0)
    lbs = jnp.cumsum(lbs, axis=0) - lbs[0]
    hg = dict(gain=norm_mix[1].reshape(1, D_MODEL), win=hgrn_w_in[0].astype(BF16),
              lb=lbs[1].reshape(1, D_MODEL), out_gain=hgrn_out_gain[0].reshape(1, D_MODEL),
              wout=hgrn_w_out[0].astype(BF16))

    def peer(x2d, l, route_tile, dense_tile):
        return _peer(x2d, pe[l]['gain'], pe[l]['wq_t'], pe[l]['kbig'], u_words, vt_words,
                     layer=l, route_tile=route_tile, dense_tile=dense_tile,
                     echunk=PEER_EXPERT_CHUNK)

    zc = jnp.zeros((bp, 2, CONV_DIM), F32)
    zkv = jnp.zeros((bp, WINDOW, KV_DIM), F32)
    rt = MIXER_ROW_TILE
    x, conv_p, k_p, v_p = _even_layer(x_prompt, zc, zkv, zkv, **ev,
                                      tile=rt, valid_rows=rt, has_cache=False)
    x = peer(x.reshape(bp * sp, D_MODEL), 0, PEER_ROUTE_TILE, PEER_DENSE_TILE)
    s0 = jnp.zeros((bp, HGRN_HEADS, HGRN_DK, HGRN_DK), F32)
    x, s_p = _hgrn_layer(x.reshape(bp, sp, D_MODEL), s0, **hg, tile=rt, valid_rows=rt)
    y_prompt = peer(x.reshape(bp * sp, D_MODEL), 1, PEER_ROUTE_TILE, PEER_DENSE_TILE)
    y_prompt = y_prompt.reshape(bp, sp, D_MODEL)

    ns = bs * ss
    xs = jnp.pad(x_sample, ((0, 0), (0, CHUNK - ss), (0, 0)))
    xs, conv_s, k_s, v_s = _even_layer(
        xs, cache_conv[0], cache_k[0].reshape(bs, WINDOW, KV_DIM),
        cache_v[0].reshape(bs, WINDOW, KV_DIM), **ev, tile=CHUNK, valid_rows=ss, has_cache=True)
    xs = peer(xs[:, :ss].reshape(ns, D_MODEL), 0, ns, ns).reshape(bs, ss, D_MODEL)
    xs = jnp.pad(xs, ((0, 0), (0, HGRN_CHUNK - ss), (0, 0)))
    xs, s_s = _hgrn_layer(xs, state_hgrn[0], **hg, tile=HGRN_CHUNK, valid_rows=ss)
    y_sample = peer(xs[:, :ss].reshape(ns, D_MODEL), 1, ns, ns).reshape(bs, ss, D_MODEL)

    kv5 = lambda a, b: a.reshape(1, b, WINDOW, N_KV_HEADS, HEAD_DIM)
    return (y_prompt, y_sample, conv_p[None], kv5(k_p, bp), kv5(v_p, bp), s_p[None],
            conv_s[None], kv5(k_s, bs), kv5(v_s, bs), s_s[None])
```

```python
import functools

import jax
import jax.numpy as jnp
import numpy as np
from jax import lax
from jax.experimental import pallas as pl
from jax.experimental.pallas import tpu as pltpu

F32 = jnp.float32
BF16 = jnp.bfloat16

D_MODEL = 1024
RMS_EPS = 1e-6
CHUNK = 64
WINDOW = 128
CONV_DIM = 512
N_Q_HEADS = 8
N_KV_HEADS = 2
HEAD_DIM = 64
ATTN_DIM = 512
KV_DIM = 128
EVEN_IN_DIM = 2304
HGRN_HEADS = 8
HGRN_DK = 128
HGRN_BLOCK = 16
HGRN_CHUNK = 128
PEER_HEADS = 8
PEER_NKEYS = 128
PEER_TOPK = 16
PEER_N_EXPERTS = PEER_NKEYS * PEER_NKEYS
LANES = 128
BF16_ROWS = 16
MXU_DIM = 256

MIXER_ROW_TILE = 256
PEER_ROUTE_TILE = 256
PEER_DENSE_TILE = 1024
PEER_EXPERT_CHUNK = 1024
PEER_PACK_BLOCK = 1024

VMEM_LIMIT_BYTES = 52 * 1024 * 1024

NEG_INF = float("-inf")


def _rms_scale(x):
    return lax.rsqrt(jnp.mean(x * x, axis=-1, keepdims=True) + RMS_EPS)


def _split_dot(x, w_bf16):
    hi = x.astype(BF16)
    lo = (x - hi.astype(F32)).astype(BF16)
    return (jnp.dot(hi, w_bf16, preferred_element_type=F32)
            + jnp.dot(lo, w_bf16, preferred_element_type=F32))


def _even_kernel(x_ref, conv0_ref, kc0_ref, vc0_ref, g_ref, win_ref, convw_ref,
                 qg_ref, kg_ref, sink_ref, hsum_q_ref, hsum_k_ref, wout_ref,
                 y_ref, nconv_ref, nk_ref, nv_ref,
                 u_scr, k_scr, v_scr, mix_scr,
                 *, tile, valid_rows, has_cache):
    t = pl.program_id(1)

    @pl.when(t == 0)
    def _():
        u_scr[0:8, :] = jnp.zeros((8, CONV_DIM), F32)
        u_scr[6:8, :] = conv0_ref[0]
        k_scr[0:WINDOW, :] = kc0_ref[0]
        v_scr[0:WINDOW, :] = vc0_ref[0]

    x = x_ref[0]
    xn = x * _rms_scale(x) * g_ref[...]
    z = jnp.dot(xn.astype(BF16), win_ref[...], preferred_element_type=F32)
    bg = z[:, 0:512]
    cg = z[:, 512:1024]
    hh = z[:, 1024:1536]
    q = z[:, 1536:2048]
    k = z[:, 2048:2176]
    v = z[:, 2176:2304]

    u = cg * hh
    u_scr[8:8 + tile, :] = u
    cw = convw_ref[...]
    conv = (cw[0:1, :] * u_scr[6:6 + tile, :] + cw[1:2, :] * u_scr[7:7 + tile, :]
            + cw[2:3, :] * u)
    mix_scr[:, 0:CONV_DIM] = bg * conv
    tail = u_scr[6 + valid_rows:8 + valid_rows, :]
    nconv_ref[0] = tail
    u_scr[6:8, :] = tail

    q_ms = _split_dot(q * q, hsum_q_ref[...]) * (1.0 / HEAD_DIM)
    q = q * lax.rsqrt(q_ms + RMS_EPS) * qg_ref[...] * (HEAD_DIM ** -0.5)
    k_ms = _split_dot(k * k, hsum_k_ref[...]) * (1.0 / HEAD_DIM)
    k = k * lax.rsqrt(k_ms + RMS_EPS) * kg_ref[...]
    k_scr[WINDOW:WINDOW + tile, :] = k
    v_scr[WINDOW:WINDOW + tile, :] = v

    lane = lax.broadcasted_iota(jnp.int32, (CHUNK, KV_DIM), 1)
    low_half = lane < HEAD_DIM
    sink = sink_ref[...]
    nkeys = WINDOW + CHUNK
    col = lax.broadcasted_iota(jnp.int32, (N_Q_HEADS * CHUNK, nkeys), 1)
    for j in range(tile // CHUNK):
        r0 = j * CHUNK
        blocks = []
        for b in range(N_Q_HEADS):
            qv = q[r0:r0 + CHUNK, (b % 4) * KV_DIM:(b % 4 + 1) * KV_DIM]
            keep = low_half if b < 4 else jnp.logical_not(low_half)
            blocks.append(jnp.where(keep, qv, 0.0))
        qs = jnp.concatenate(blocks, axis=0).astype(BF16)
        kw = k_scr[r0:r0 + nkeys, :].astype(BF16)
        vw = v_scr[r0:r0 + nkeys, :].astype(BF16)
        s = lax.dot_general(qs, kw, (((1,), (1,)), ((), ())),
                            preferred_element_type=F32)
        if valid_rows < tile:
            s = jnp.where(col < WINDOW + valid_rows, s, NEG_INF)
        if not has_cache and r0 < WINDOW:
            s = jnp.where(jnp.logical_or(col >= WINDOW - r0, t > 0), s, NEG_INF)
        m = jnp.maximum(jnp.max(s, axis=-1, keepdims=True), sink)
        p = jnp.exp(s - m)
        p = p / (jnp.sum(p, axis=-1, keepdims=True) + jnp.exp(sink - m))
        o = jnp.dot(p.astype(BF16), vw, preferred_element_type=F32)
        for jj in range(4):
            oj = jnp.where(low_half, o[jj * CHUNK:(jj + 1) * CHUNK, :],
                           o[(4 + jj) * CHUNK:(5 + jj) * CHUNK, :])
            mix_scr[r0:r0 + CHUNK, CONV_DIM + jj * KV_DIM:CONV_DIM + (jj + 1) * KV_DIM] = oj

    nk = k_scr[valid_rows:valid_rows + WINDOW, :]
    nv = v_scr[valid_rows:valid_rows + WINDOW, :]
    nk_ref[0] = nk
    nv_ref[0] = nv
    k_scr[0:WINDOW, :] = nk
    v_scr[0:WINDOW, :] = nv

    mix = jnp.dot(mix_scr[...].astype(BF16), wout_ref[...], preferred_element_type=F32)
    y_ref[0] = x + mix


def _even_layer(x, conv0, kc0, vc0, gain, win, convw, qg, kg, sink_rows, hsum_q, hsum_k, wout,
                *, tile, valid_rows, has_cache):
    b, s, _ = x.shape
    nt = s // tile
    const2 = lambda i, j: (0, 0)
    per_b = lambda i, j: (i, 0, 0)
    kern = functools.partial(_even_kernel, tile=tile, valid_rows=valid_rows, has_cache=has_cache)
    return pl.pallas_call(
        kern,
        grid=(b, nt),
        in_specs=[
            pl.BlockSpec((1, tile, D_MODEL), lambda i, j: (i, j, 0)),
            pl.BlockSpec((1, 2, CONV_DIM), per_b),
            pl.BlockSpec((1, WINDOW, KV_DIM), per_b),
            pl.BlockSpec((1, WINDOW, KV_DIM), per_b),
            pl.BlockSpec((1, D_MODEL), const2),
            pl.BlockSpec((D_MODEL, EVEN_IN_DIM), const2),
            pl.BlockSpec((3, CONV_DIM), const2),
            pl.BlockSpec((1, ATTN_DIM), const2),
            pl.BlockSpec((1, KV_DIM), const2),
            pl.BlockSpec((N_Q_HEADS * CHUNK, 1), const2),
            pl.BlockSpec((ATTN_DIM, ATTN_DIM), const2),
            pl.BlockSpec((KV_DIM, KV_DIM), const2),
            pl.BlockSpec((D_MODEL, D_MODEL), const2),
        ],
        out_specs=[
            pl.BlockSpec((1, tile, D_MODEL), lambda i, j: (i, j, 0)),
            pl.BlockSpec((1, 2, CONV_DIM), per_b),
            pl.BlockSpec((1, WINDOW, KV_DIM), per_b),
            pl.BlockSpec((1, WINDOW, KV_DIM), per_b),
        ],
        out_shape=[
            jax.ShapeDtypeStruct((b, s, D_MODEL), F32),
            jax.ShapeDtypeStruct((b, 2, CONV_DIM), F32),
            jax.ShapeDtypeStruct((b, WINDOW, KV_DIM), F32),
            jax.ShapeDtypeStruct((b, WINDOW, KV_DIM), F32),
        ],
        scratch_shapes=[
            pltpu.VMEM((8 + tile, CONV_DIM), F32),
            pltpu.VMEM((WINDOW + tile, KV_DIM), F32),
            pltpu.VMEM((WINDOW + tile, KV_DIM), F32),
            pltpu.VMEM((tile, D_MODEL), F32),
        ],
        compiler_params=pltpu.CompilerParams(
            dimension_semantics=("arbitrary", "arbitrary"),
            vmem_limit_bytes=VMEM_LIMIT_BYTES),
        name="even_layer",
    )(x, conv0, kc0, vc0, gain, win, convw, qg, kg, sink_rows, hsum_q, hsum_k, wout)


def _hgrn_kernel(x_ref, s0_ref, g_ref, win_ref, lb_ref, og_ref, wout_ref,
                 y_ref, snew_ref,
                 s_scr, q_scr, k_scr, v_scr, lf_scr, o_scr,
                 *, tile, valid_rows):
    t = pl.program_id(1)

    @pl.when(t == 0)
    def _():
        s_scr[...] = s0_ref[0]

    x = x_ref[0]
    xn = x * _rms_scale(x) * g_ref[...]
    z = jnp.dot(xn.astype(BF16), win_ref[...], preferred_element_type=F32)
    lb = lb_ref[...]
    fg = lb + (1.0 - lb) * jax.nn.sigmoid(z[:, D_MODEL:2 * D_MODEL])
    logf = jnp.log(fg)
    kk = 1.0 - fg
    if valid_rows < tile:
        row = lax.broadcasted_iota(jnp.int32, (tile, D_MODEL), 0)
        live = row < valid_rows
        logf = jnp.where(live, logf, 0.0)
        kk = jnp.where(live, kk, 0.0)
    q_scr[...] = z[:, 0:D_MODEL]
    k_scr[...] = kk
    v_scr[...] = z[:, 2 * D_MODEL:3 * D_MODEL]
    gate = z[:, 3 * D_MODEL:4 * D_MODEL]

    ri = lax.broadcasted_iota(jnp.int32, (tile, tile), 0)
    ci = lax.broadcasted_iota(jnp.int32, (tile, tile), 1)
    same = (ri // HGRN_CHUNK) == (ci // HGRN_CHUNK)
    tril = jnp.where(jnp.logical_and(same, ci <= ri), 1.0, 0.0).astype(BF16)
    hi = logf.astype(BF16)
    lo = (logf - hi.astype(F32)).astype(BF16)
    lf_scr[...] = (jnp.dot(tril, hi, preferred_element_type=F32)
                   + jnp.dot(tril, lo, preferred_element_type=F32))

    pair_blk = jnp.where(
        lax.broadcasted_iota(jnp.int32, (2 * HGRN_BLOCK, 2 * HGRN_DK), 0) // HGRN_BLOCK
        == lax.broadcasted_iota(jnp.int32, (2 * HGRN_BLOCK, 2 * HGRN_DK), 1) // HGRN_DK,
        1.0, 0.0).astype(BF16)

    for c in range(tile // HGRN_CHUNK):
        c0 = c * HGRN_CHUNK
        g = lf_scr[c0:c0 + HGRN_CHUNK, :]
        gtot = lf_scr[c0 + HGRN_CHUNK - 1:c0 + HGRN_CHUNK, :]
        qe = (q_scr[c0:c0 + HGRN_CHUNK, :] * jnp.exp(g)).astype(BF16)
        kh = k_scr[c0:c0 + HGRN_CHUNK, :] * jnp.exp(gtot - g)
        kh_t = kh.T.astype(BF16)
        dec_t = jnp.broadcast_to(jnp.exp(gtot), (HGRN_CHUNK, D_MODEL)).T
        vc = v_scr[c0:c0 + HGRN_CHUNK, :].astype(BF16)
        for h in range(HGRN_HEADS):
            hs = slice(h * HGRN_DK, (h + 1) * HGRN_DK)
            s_h = s_scr[h]
            o_scr[c0:c0 + HGRN_CHUNK, hs] = jnp.dot(
                qe[:, hs], s_h.astype(BF16), preferred_element_type=F32)
            s_scr[h] = dec_t[hs, :] * s_h + jnp.dot(
                kh_t[hs, :], vc[:, hs], preferred_element_type=F32)
        for j in range(HGRN_CHUNK // HGRN_BLOCK):
            r0 = c0 + j * HGRN_BLOCK
            nrow = HGRN_CHUNK - j * HGRN_BLOCK
            gj = lf_scr[r0:c0 + HGRN_CHUNK, :]
            if j == 0:
                rel = gj
            else:
                rel = gj - lf_scr[r0 - 1:r0, :]
            qj = (q_scr[r0:c0 + HGRN_CHUNK, :] * jnp.exp(rel)).astype(BF16)
            kj = (k_scr[r0:r0 + HGRN_BLOCK, :] * jnp.exp(-rel[0:HGRN_BLOCK, :])).astype(BF16)
            vj = v_scr[r0:r0 + HGRN_BLOCK, :].astype(BF16)
            causal = (lax.broadcasted_iota(jnp.int32, (nrow, 2 * HGRN_BLOCK), 0)
                      >= lax.broadcasted_iota(jnp.int32, (nrow, 2 * HGRN_BLOCK), 1) % HGRN_BLOCK)
            for hp in range(HGRN_HEADS // 2):
                ps = slice(2 * hp * HGRN_DK, (2 * hp + 2) * HGRN_DK)
                k2 = jnp.concatenate([kj[:, ps], kj[:, ps]], axis=0) * pair_blk
                v2 = jnp.concatenate([vj[:, ps], vj[:, ps]], axis=0) * pair_blk
                a = lax.dot_general(qj[:, ps], k2, (((1,), (1,)), ((), ())),
                                    preferred_element_type=F32)
                a = jnp.where(causal, a, 0.0).astype(BF16)
                o_scr[r0:c0 + HGRN_CHUNK, ps] += jnp.dot(a, v2, preferred_element_type=F32)

    snew_ref[0] = s_scr[...]
    o = o_scr[...]
    o = o * _rms_scale(o) * og_ref[...]
    o = o * (gate * jax.nn.sigmoid(gate))
    y_ref[0] = x + jnp.dot(o.astype(BF16), wout_ref[...], preferred_element_type=F32)


def _hgrn_layer(x, s0, gain, win, lb, out_gain, wout, *, tile, valid_rows):
    b, s, _ = x.shape
    nt = s // tile
    const2 = lambda i, j: (0, 0)
    kern = functools.partial(_hgrn_kernel, tile=tile, valid_rows=valid_rows)
    return pl.pallas_call(
        kern,
        grid=(b, nt),
        in_specs=[
            pl.BlockSpec((1, tile, D_MODEL), lambda i, j: (i, j, 0)),
            pl.BlockSpec((1, HGRN_HEADS, HGRN_DK, HGRN_DK), lambda i, j: (i, 0, 0, 0)),
            pl.BlockSpec((1, D_MODEL), const2),
            pl.BlockSpec((D_MODEL, 4 * D_MODEL), const2),
            pl.BlockSpec((1, D_MODEL), const2),
            pl.BlockSpec((1, D_MODEL), const2),
            pl.BlockSpec((D_MODEL, D_MODEL), const2),
        ],
        out_specs=[
            pl.BlockSpec((1, tile, D_MODEL), lambda i, j: (i, j, 0)),
            pl.BlockSpec((1, HGRN_HEADS, HGRN_DK, HGRN_DK), lambda i, j: (i, 0, 0, 0)),
        ],
        out_shape=[
            jax.ShapeDtypeStruct((b, s, D_MODEL), F32),
            jax.ShapeDtypeStruct((b, HGRN_HEADS, HGRN_DK, HGRN_DK), F32),
        ],
        scratch_shapes=[
            pltpu.VMEM((HGRN_HEADS, HGRN_DK, HGRN_DK), F32),
            pltpu.VMEM((tile, D_MODEL), F32),
            pltpu.VMEM((tile, D_MODEL), F32),
            pltpu.VMEM((tile, D_MODEL), F32),
            pltpu.VMEM((tile, D_MODEL), F32),
            pltpu.VMEM((tile, D_MODEL), F32),
        ],
        compiler_params=pltpu.CompilerParams(
            dimension_semantics=("arbitrary", "arbitrary"),
            vmem_limit_bytes=VMEM_LIMIT_BYTES),
        name="hgrn_layer",
    )(x, s0, gain, win, lb, out_gain, wout)


def _sort_network(n):
    pairs = []

    def merge(lo, hi, r):
        step = r * 2
        if step < hi - lo:
            merge(lo, hi, step)
            merge(lo + r, hi, step)
            pairs.extend((i, i + r) for i in range(lo + r, hi - r, step))
        else:
            pairs.append((lo, lo + r))

    def sort(lo, hi):
        if hi - lo >= 1:
            mid = lo + (hi - lo) // 2
            sort(lo, mid)
            sort(mid + 1, hi)
            merge(lo, hi, 1)

    sort(0, n - 1)
    return pairs


_SORT16 = _sort_network(PEER_TOPK)


def _sort16_desc(v):
    v = list(v)
    for i, j in _SORT16:
        v[i], v[j] = jnp.maximum(v[i], v[j]), jnp.minimum(v[i], v[j])
    return v


def _merge_top16(a, b):
    c = [jnp.maximum(a[i], b[PEER_TOPK - 1 - i]) for i in range(PEER_TOPK)]
    for d in (8, 4, 2, 1):
        for i in range(PEER_TOPK):
            if not i & d:
                c[i], c[i + d] = jnp.maximum(c[i], c[i + d]), jnp.minimum(c[i], c[i + d])
    return c


def _top16_sorted(load_slab):
    lists = [_sort16_desc([load_slab(PEER_TOPK * g + i) for i in range(PEER_TOPK)])
             for g in range(PEER_NKEYS // PEER_TOPK)]
    while len(lists) > 1:
        lists = [_merge_top16(a, b) for a, b in zip(lists[0::2], lists[1::2])]
    return lists[0]


def _bf16_pair_words(x):
    bits = lax.bitcast_convert_type(x.astype(BF16).astype(F32), jnp.int32)
    return bits | lax.shift_right_logical(bits, 16)


def _route_kernel(x_ref, g_ref, wq_ref, kb_ref,
                  xnt_ref, q2h_ref, e2h_ref, thr_ref, e1_ref,
                  s1_scr, s2_scr, a_scr, b_scr, cand_scr,
                  *, tile):
    x = x_ref[...]
    xn = x * _rms_scale(x) * g_ref[...]
    xnt = xn.T.astype(BF16)
    xnt_ref[...] = pltpu.bitcast(xnt, jnp.int32)
    qt = jnp.dot(wq_ref[...], xnt, preferred_element_type=F32).astype(BF16)
    half = PEER_HEADS * 128
    s1 = jnp.dot(kb_ref[0], qt[0:half, :], preferred_element_type=F32)
    s2 = jnp.dot(kb_ref[1], qt[half:2 * half, :], preferred_element_type=F32)
    s1_scr[...] = s1.reshape(PEER_NKEYS, PEER_HEADS, tile)
    s2_scr[...] = s2.reshape(PEER_NKEYS, PEER_HEADS, tile)
    for lt in range(tile // LANES):
        ls = slice(lt * LANES, (lt + 1) * LANES)
        for src, dst in ((s1_scr, a_scr), (s2_scr, b_scr)):
            top = _top16_sorted(lambda kidx, src=src: src[kidx, :, ls])
            for i in range(PEER_TOPK):
                dst[i, :, ls] = top[i]

    pairs = [(i, j) for i in range(PEER_TOPK) for j in range(PEER_TOPK)
             if (i + 1) * (j + 1) <= PEER_TOPK]
    rows = [[] for _ in range(PEER_TOPK)]
    for n, (i, j) in enumerate(pairs):
        c = a_scr[i] + b_scr[j]
        cand_scr[n] = c
        rows[i].append(c)
    s_a = _sort16_desc(rows[1] + [rows[i][0] for i in range(8, PEER_TOPK)])
    s_b = _sort16_desc(rows[2] + rows[3] + rows[4] + rows[5] + rows[6])
    top = _merge_top16(_merge_top16(rows[0], s_a), s_b)
    tau = jnp.minimum(top[13], jnp.minimum(jnp.maximum(top[14], rows[7][1]),
                                           jnp.maximum(top[15], rows[7][0])))

    a0 = a_scr[0]
    b0 = b_scr[0]
    zsum = jnp.zeros((PEER_HEADS, tile), F32)
    codes = []
    for i in range(PEER_TOPK):
        lam = jnp.zeros((PEER_HEADS, tile), F32)
        ea = jnp.exp(a_scr[i] - a0)
        for n, (pi, pj) in enumerate(pairs):
            if pi != i:
                continue
            sel = cand_scr[n] >= tau
            lam = lam + jnp.where(sel, 1.0, 0.0)
            zsum = zsum + jnp.where(sel, ea * jnp.exp(b_scr[pj] - b0), 0.0)
        codes.append((PEER_TOPK + 1.0) - lam)

    s1 = s1_scr[...]
    thr = jnp.full((PEER_NKEYS, PEER_HEADS, tile), PEER_TOPK + 1.0, F32)
    for i in range(PEER_TOPK):
        thr = jnp.where(s1 == a_scr[i][None], codes[i][None], thr)
    thr_w = _bf16_pair_words(thr).reshape(PEER_NKEYS * PEER_HEADS, tile)
    e1_w = _bf16_pair_words(jnp.exp(s1 - a0[None])).reshape(PEER_NKEYS * PEER_HEADS, tile)
    for lt in range(tile // LANES):
        thr_ref[lt] = thr_w[:, lt * LANES:(lt + 1) * LANES]
        e1_ref[lt] = e1_w[:, lt * LANES:(lt + 1) * LANES]

    s2hm = jnp.dot(kb_ref[2], qt[half:2 * half, :], preferred_element_type=F32)
    zscale = 0.5 / zsum
    for h in range(PEER_HEADS):
        s2h = s2hm[h * PEER_NKEYS:(h + 1) * PEER_NKEYS, :]
        q2 = jnp.zeros((PEER_NKEYS, tile), F32)
        for jj in range(PEER_TOPK - 1, -1, -1):
            q2 = jnp.where(s2h >= b_scr[jj, h:h + 1, :], float(PEER_TOPK - jj), q2)
        q2h_ref[h] = pltpu.bitcast(q2.astype(BF16), jnp.int32)
        e2 = jnp.where(q2 > 0.0, jnp.exp(s2h - b0[h:h + 1, :]) * zscale[h:h + 1, :], 0.0)
        e2h_ref[h] = pltpu.bitcast(e2.astype(BF16), jnp.int32)


def _peer_route(x, gain, wq_t, kbig, *, tile):
    n = x.shape[0]
    nt = n // tile
    npairs = sum(1 for i in range(PEER_TOPK) for j in range(PEER_TOPK)
                 if (i + 1) * (j + 1) <= PEER_TOPK)
    rows = PEER_NKEYS * PEER_HEADS
    kern = functools.partial(_route_kernel, tile=tile)
    return pl.pallas_call(
        kern,
        grid=(nt,),
        in_specs=[
            pl.BlockSpec((tile, D_MODEL), lambda i: (i, 0)),
            pl.BlockSpec((1, D_MODEL), lambda i: (0, 0)),
            pl.BlockSpec((2 * rows, D_MODEL), lambda i: (0, 0)),
            pl.BlockSpec((3, rows, rows), lambda i: (0, 0, 0)),
        ],
        out_specs=[
            pl.BlockSpec((D_MODEL // 2, tile), lambda i: (0, i)),
            pl.BlockSpec((PEER_HEADS, PEER_NKEYS // 2, tile), lambda i: (0, 0, i)),
            pl.BlockSpec((PEER_HEADS, PEER_NKEYS // 2, tile), lambda i: (0, 0, i)),
            pl.BlockSpec((tile // LANES, rows, LANES), lambda i: (i, 0, 0)),
            pl.BlockSpec((tile // LANES, rows, LANES), lambda i: (i, 0, 0)),
        ],
        out_shape=[
            jax.ShapeDtypeStruct((D_MODEL // 2, n), jnp.int32),
            jax.ShapeDtypeStruct((PEER_HEADS, PEER_NKEYS // 2, n), jnp.int32),
            jax.ShapeDtypeStruct((PEER_HEADS, PEER_NKEYS // 2, n), jnp.int32),
            jax.ShapeDtypeStruct((n // LANES, rows, LANES), jnp.int32),
            jax.ShapeDtypeStruct((n // LANES, rows, LANES), jnp.int32),
        ],
        scratch_shapes=[
            pltpu.VMEM((PEER_NKEYS, PEER_HEADS, tile), F32),
            pltpu.VMEM((PEER_NKEYS, PEER_HEADS, tile), F32),
            pltpu.VMEM((PEER_TOPK, PEER_HEADS, tile), F32),
            pltpu.VMEM((PEER_TOPK, PEER_HEADS, tile), F32),
            pltpu.VMEM((npairs, PEER_HEADS, tile), F32),
        ],
        compiler_params=pltpu.CompilerParams(
            dimension_semantics=("arbitrary",),
            vmem_limit_bytes=VMEM_LIMIT_BYTES),
        name="peer_route",
    )(x, gain, wq_t, kbig)


def _dense_kernel(x_ref, xnt_ref, q2h_ref, e2h_ref, thr_ref, e1_ref, u_ref, vt_ref,
                  y_ref, acc_ref, w_scr, *, tile, echunk):
    j = pl.program_id(1)

    @pl.when(j == 0)
    def _():
        acc_ref[...] = jnp.zeros_like(acc_ref)

    mxu_w = min(tile, MXU_DIM)
    ngrp = PEER_NKEYS // BF16_ROWS

    def bcast_row(word_ref, r, l0):
        row = word_ref[l0 // LANES, r:r + 1, :]
        return pltpu.bitcast(jnp.broadcast_to(row, (8, LANES)), BF16)

    h_all = jnp.dot(pltpu.bitcast(u_ref[0], BF16), pltpu.bitcast(xnt_ref[...], BF16),
                         preferred_element_type=F32)
    na = echunk // PEER_NKEYS
    ablk = 4
    gblk = 4
    for mb in range(tile // mxu_w):
        m0 = mb * mxu_w
        for lb in range(mxu_w // LANES):
            l0 = m0 + lb * LANES
            for gb in range(ngrp // gblk):
                grp = [gb * gblk + g for g in range(gblk)]
                for ab in range(na // ablk):
                    keys = [ab * ablk + a for a in range(ablk)]
                    gsum = [[jnp.zeros((BF16_ROWS, LANES), BF16) for _ in grp] for _ in keys]
                    for h in range(PEER_HEADS):
                        q2 = [pltpu.bitcast(q2h_ref[h, 8 * g:8 * g + 8, l0:l0 + LANES], BF16)
                              for g in grp]
                        e2 = [pltpu.bitcast(e2h_ref[h, 8 * g:8 * g + 8, l0:l0 + LANES], BF16)
                              for g in grp]
                        for ai, a in enumerate(keys):
                            thr = bcast_row(thr_ref, a * PEER_HEADS + h, l0)
                            e1 = bcast_row(e1_ref, a * PEER_HEADS + h, l0)
                            for g in range(gblk):
                                hit = q2[g] >= thr
                                gsum[ai][g] = gsum[ai][g] + jnp.where(hit, e2[g], 0.0) * e1
                    for ai, a in enumerate(keys):
                        for g in range(gblk):
                            e0 = a * PEER_NKEYS + grp[g] * BF16_ROWS
                            hv = h_all[e0:e0 + BF16_ROWS, l0:l0 + LANES]
                            act = (hv * (1.0 + lax.erf(hv * (2.0 ** -0.5)))).astype(BF16)
                            w_scr[e0:e0 + BF16_ROWS, l0:l0 + LANES] = (
                                jnp.where(gsum[ai][g] > 0.0, act, 0.0) * gsum[ai][g])
        acc_ref[:, m0:m0 + mxu_w] += jnp.dot(
            pltpu.bitcast(vt_ref[0], BF16), w_scr[:, m0:m0 + mxu_w],
            preferred_element_type=F32)

    @pl.when(j == pl.num_programs(1) - 1)
    def _():
        y_ref[...] = x_ref[...] + acc_ref[...].T


def _peer_dense(x, xnt, q2h, e2h, thr, e1, u_words, vt_words, *, layer, tile, echunk):
    n = x.shape[0]
    nt = n // tile
    nchunk = PEER_N_EXPERTS // echunk
    crows = (echunk // PEER_NKEYS) * PEER_HEADS
    kern = functools.partial(_dense_kernel, tile=tile, echunk=echunk)
    return pl.pallas_call(
        kern,
        grid=(nt, nchunk),
        in_specs=[
            pl.BlockSpec((tile, D_MODEL), lambda i, j: (i, 0)),
            pl.BlockSpec((D_MODEL // 2, tile), lambda i, j: (0, i)),
            pl.BlockSpec((PEER_HEADS, PEER_NKEYS // 2, tile), lambda i, j: (0, 0, i)),
            pl.BlockSpec((PEER_HEADS, PEER_NKEYS // 2, tile), lambda i, j: (0, 0, i)),
            pl.BlockSpec((tile // LANES, crows, LANES), lambda i, j: (i, j, 0)),
            pl.BlockSpec((tile // LANES, crows, LANES), lambda i, j: (i, j, 0)),
            pl.BlockSpec((1, echunk // 2, D_MODEL), lambda i, j: (layer, j, 0)),
            pl.BlockSpec((1, D_MODEL // 2, echunk), lambda i, j: (layer, 0, j)),
        ],
        out_specs=pl.BlockSpec((tile, D_MODEL), lambda i, j: (i, 0)),
        out_shape=jax.ShapeDtypeStruct((n, D_MODEL), F32),
        scratch_shapes=[pltpu.VMEM((D_MODEL, tile), F32),
                        pltpu.VMEM((echunk, tile), BF16)],
        compiler_params=pltpu.CompilerParams(
            dimension_semantics=("arbitrary", "arbitrary"),
            vmem_limit_bytes=VMEM_LIMIT_BYTES),
        name="peer_dense",
    )(x, xnt, q2h, e2h, thr, e1, u_words, vt_words)


def _peer(x, gain, wq_t, kbig, u_words, vt_words, *, layer, route_tile, dense_tile, echunk):
    xnt, q2h, e2h, thr, e1 = _peer_route(x, gain, wq_t, kbig, tile=route_tile)
    return _peer_dense(x, xnt, q2h, e2h, thr, e1, u_words, vt_words, layer=layer,
                       tile=dense_tile, echunk=echunk)


def _even_params(norm_g, w_in, conv_w, q_gain, k_gain, sinks, w_out):
    qcols = np.array([1536 + (j + 4 * hf) * HEAD_DIM + d
                      for j in range(4) for hf in range(2) for d in range(HEAD_DIM)])
    cols = np.concatenate([np.arange(1536), qcols, np.arange(2048, EVEN_IN_DIM)])
    orow = np.array([CONV_DIM + (j + 4 * hf) * HEAD_DIM + d
                     for j in range(4) for hf in range(2) for d in range(HEAD_DIM)])
    rows = np.concatenate([np.arange(CONV_DIM), orow])
    blk = lambda n: jnp.asarray(
        (np.arange(n)[:, None] // HEAD_DIM) == (np.arange(n)[None, :] // HEAD_DIM)).astype(BF16)
    return dict(
        gain=norm_g.reshape(1, D_MODEL),
        win=w_in[:, cols].astype(BF16),
        convw=conv_w,
        qg=jnp.tile(q_gain, N_Q_HEADS).reshape(1, ATTN_DIM),
        kg=jnp.tile(k_gain, N_KV_HEADS).reshape(1, KV_DIM),
        sink_rows=jnp.repeat(sinks, CHUNK).reshape(N_Q_HEADS * CHUNK, 1),
        hsum_q=blk(ATTN_DIM),
        hsum_k=blk(KV_DIM),
        wout=w_out[rows, :].astype(BF16),
    )


def _pack_tables_kernel(u_ref, v_ref, uo_ref, vo_ref):
    uo_ref[0] = pltpu.bitcast(u_ref[0].astype(BF16), jnp.int32)
    vo_ref[0] = pltpu.bitcast(v_ref[0].T.astype(BF16), jnp.int32)


def _pack_tables(u_tab, v_tab, *, eblk):
    nl, ne, d = u_tab.shape
    return pl.pallas_call(
        _pack_tables_kernel,
        grid=(nl, ne // eblk),
        in_specs=[pl.BlockSpec((1, eblk, d), lambda l, i: (l, i, 0)),
                  pl.BlockSpec((1, eblk, d), lambda l, i: (l, i, 0))],
        out_specs=[pl.BlockSpec((1, eblk // 2, d), lambda l, i: (l, i, 0)),
                   pl.BlockSpec((1, d // 2, eblk), lambda l, i: (l, 0, i))],
        out_shape=[jax.ShapeDtypeStruct((nl, ne // 2, d), jnp.int32),
                   jax.ShapeDtypeStruct((nl, d // 2, ne), jnp.int32)],
        compiler_params=pltpu.CompilerParams(
            dimension_semantics=("arbitrary", "arbitrary"),
            vmem_limit_bytes=VMEM_LIMIT_BYTES),
        name="pack_tables",
    )(u_tab, v_tab)


def _peer_params(norm_g, w_query, sub_keys):
    wq_t = w_query.T.reshape(PEER_HEADS, 2, 128, D_MODEL).transpose(1, 0, 2, 3)
    wq_t = wq_t.reshape(2 * PEER_HEADS * 128, D_MODEL).astype(BF16)
    eye = jnp.eye(PEER_HEADS, dtype=sub_keys.dtype)
    kbig = jnp.einsum('hpkd,hg->pkhgd', sub_keys, eye).reshape(
        2, PEER_NKEYS * PEER_HEADS, PEER_HEADS * 128)
    khm = jnp.einsum('hkd,hg->hkgd', sub_keys[:, 1], eye).reshape(
        1, PEER_HEADS * PEER_NKEYS, PEER_HEADS * 128)
    kbig = jnp.concatenate([kbig, khm], axis=0).astype(BF16)
    return dict(gain=norm_g.reshape(1, D_MODEL), wq_t=wq_t, kbig=kbig)


def kernel(x_prompt, x_sample, cache_conv, cache_k, cache_v, state_hgrn, norm_mix, norm_ffn,
           even_w_in, even_conv_w, even_q_gain, even_k_gain, even_sinks, even_w_out,
           hgrn_w_in, hgrn_lb, hgrn_out_gain, hgrn_w_out,
           peer_w_query, peer_sub_keys, peer_u, peer_v):
    bp, sp, _ = x_prompt.shape
    bs, ss, _ = x_sample.shape

    ev = _even_params(norm_mix[0], even_w_in[0], even_conv_w[0], even_q_gain[0],
                      even_k_gain[0], even_sinks[0], even_w_out[0])
    u_words, vt_words = _pack_tables(peer_u, peer_v, eblk=PEER_PACK_BLOCK)
    pe = [_peer_params(norm_ffn[l], peer_w_query[l], peer_sub_keys[l]) for l in range(2)]
    lbs = jax.nn.softmax(hgrn_lb.astype(F32), axis=0)
    lbs = jnp.cumsum(lbs, axis=0) - lbs[0]
    hg = dict(gain=norm_mix[1].reshape(1, D_MODEL), win=hgrn_w_in[0].astype(BF16),
              lb=lbs[1].reshape(1, D_MODEL), out_gain=hgrn_out_gain[0].reshape(1, D_MODEL),
              wout=hgrn_w_out[0].astype(BF16))

    def peer(x2d, l, route_tile, dense_tile):
        return _peer(x2d, pe[l]['gain'], pe[l]['wq_t'], pe[l]['kbig'], u_words, vt_words,
                     layer=l, route_tile=route_tile, dense_tile=dense_tile,
                     echunk=PEER_EXPERT_CHUNK)

    zc = jnp.zeros((bp, 2, CONV_DIM), F32)
    zkv = jnp.zeros((bp, WINDOW, KV_DIM), F32)
    rt = MIXER_ROW_TILE
    x, conv_p, k_p, v_p = _even_layer(x_prompt, zc, zkv, zkv, **ev,
                                      tile=rt, valid_rows=rt, has_cache=False)
    x = peer(x.reshape(bp * sp, D_MODEL), 0, PEER_ROUTE_TILE, PEER_DENSE_TILE)
    s0 = jnp.zeros((bp, HGRN_HEADS, HGRN_DK, HGRN_DK), F32)
    x, s_p = _hgrn_layer(x.reshape(bp, sp, D_MODEL), s0, **hg, tile=rt, valid_rows=rt)
    y_prompt = peer(x.reshape(bp * sp, D_MODEL), 1, PEER_ROUTE_TILE, PEER_DENSE_TILE)
    y_prompt = y_prompt.reshape(bp, sp, D_MODEL)

    ns = bs * ss
    xs = jnp.pad(x_sample, ((0, 0), (0, CHUNK - ss), (0, 0)))
    xs, conv_s, k_s, v_s = _even_layer(
        xs, cache_conv[0], cache_k[0].reshape(bs, WINDOW, KV_DIM),
        cache_v[0].reshape(bs, WINDOW, KV_DIM), **ev, tile=CHUNK, valid_rows=ss, has_cache=True)
    xs = peer(xs[:, :ss].reshape(ns, D_MODEL), 0, ns, ns).reshape(bs, ss, D_MODEL)
    xs = jnp.pad(xs, ((0, 0), (0, HGRN_CHUNK - ss), (0, 0)))
    xs, s_s = _hgrn_layer(xs, state_hgrn[0], **hg, tile=HGRN_CHUNK, valid_rows=ss)
    y_sample = peer(xs[:, :ss].reshape(ns, D_MODEL), 1, ns, ns).reshape(bs, ss, D_MODEL)

    kv5 = lambda a, b: a.reshape(1, b, WINDOW, N_KV_HEADS, HEAD_DIM)
    return (y_prompt, y_sample, conv_p[None], kv5(k_p, bp), kv5(v_p, bp), s_p[None],
            conv_s[None], kv5(k_s, bs), kv5(v_s, bs), s_s[None])
```

```python
import functools

import jax
import jax.numpy as jnp
import numpy as np
from jax import lax
from jax.experimental import pallas as pl
from jax.experimental.pallas import tpu as pltpu

F32 = jnp.float32
BF16 = jnp.bfloat16

D_MODEL = 1024
RMS_EPS = 1e-6
CHUNK = 64
WINDOW = 128
CONV_DIM = 512
N_Q_HEADS = 8
N_KV_HEADS = 2
HEAD_DIM = 64
ATTN_DIM = 512
KV_DIM = 128
EVEN_IN_DIM = 2304
HGRN_HEADS = 8
HGRN_DK = 128
HGRN_BLOCK = 16
HGRN_CHUNK = 128
PEER_HEADS = 8
PEER_NKEYS = 128
PEER_TOPK = 16
PEER_N_EXPERTS = PEER_NKEYS * PEER_NKEYS
LANES = 128
BF16_ROWS = 16
MXU_DIM = 256

MIXER_ROW_TILE = 256
PEER_ROUTE_TILE = 512
PEER_DENSE_TILE = 1024
PEER_EXPERT_CHUNK = 1024
PEER_PACK_BLOCK = 1024

VMEM_LIMIT_BYTES = 52 * 1024 * 1024

NEG_INF = float("-inf")


def _rms_scale(x):
    return lax.rsqrt(jnp.mean(x * x, axis=-1, keepdims=True) + RMS_EPS)


def _split_dot(x, w_bf16):
    hi = x.astype(BF16)
    lo = (x - hi.astype(F32)).astype(BF16)
    return (jnp.dot(hi, w_bf16, preferred_element_type=F32)
            + jnp.dot(lo, w_bf16, preferred_element_type=F32))


def _even_kernel(x_ref, conv0_ref, kc0_ref, vc0_ref, g_ref, win_ref, convw_ref,
                 qg_ref, kg_ref, sink_ref, hsum_q_ref, hsum_k_ref, wout_ref,
                 y_ref, nconv_ref, nk_ref, nv_ref,
                 u_scr, k_scr, v_scr, mix_scr,
                 *, tile, valid_rows, has_cache):
    t = pl.program_id(1)

    @pl.when(t == 0)
    def _():
        u_scr[0:8, :] = jnp.zeros((8, CONV_DIM), F32)
        u_scr[6:8, :] = conv0_ref[0]
        k_scr[0:WINDOW, :] = kc0_ref[0]
        v_scr[0:WINDOW, :] = vc0_ref[0]

    x = x_ref[0]
    xn = x * _rms_scale(x) * g_ref[...]
    z = jnp.dot(xn.astype(BF16), win_ref[...], preferred_element_type=F32)
    bg = z[:, 0:512]
    cg = z[:, 512:1024]
    hh = z[:, 1024:1536]
    q = z[:, 1536:2048]
    k = z[:, 2048:2176]
    v = z[:, 2176:2304]

    u = cg * hh
    u_scr[8:8 + tile, :] = u
    cw = convw_ref[...]
    conv = (cw[0:1, :] * u_scr[6:6 + tile, :] + cw[1:2, :] * u_scr[7:7 + tile, :]
            + cw[2:3, :] * u)
    mix_scr[:, 0:CONV_DIM] = bg * conv
    tail = u_scr[6 + valid_rows:8 + valid_rows, :]
    nconv_ref[0] = tail
    u_scr[6:8, :] = tail

    q_ms = _split_dot(q * q, hsum_q_ref[...]) * (1.0 / HEAD_DIM)
    q = q * lax.rsqrt(q_ms + RMS_EPS) * qg_ref[...] * (HEAD_DIM ** -0.5)
    k_ms = _split_dot(k * k, hsum_k_ref[...]) * (1.0 / HEAD_DIM)
    k = k * lax.rsqrt(k_ms + RMS_EPS) * kg_ref[...]
    k_scr[WINDOW:WINDOW + tile, :] = k
    v_scr[WINDOW:WINDOW + tile, :] = v

    lane = lax.broadcasted_iota(jnp.int32, (CHUNK, KV_DIM), 1)
    low_half = lane < HEAD_DIM
    sink = sink_ref[...]
    nkeys = WINDOW + CHUNK
    col = lax.broadcasted_iota(jnp.int32, (N_Q_HEADS * CHUNK, nkeys), 1)
    for j in range(tile // CHUNK):
        r0 = j * CHUNK
        blocks = []
        for b in range(N_Q_HEADS):
            qv = q[r0:r0 + CHUNK, (b % 4) * KV_DIM:(b % 4 + 1) * KV_DIM]
            keep = low_half if b < 4 else jnp.logical_not(low_half)
            blocks.append(jnp.where(keep, qv, 0.0))
        qs = jnp.concatenate(blocks, axis=0).astype(BF16)
        kw = k_scr[r0:r0 + nkeys, :].astype(BF16)
        vw = v_scr[r0:r0 + nkeys, :].astype(BF16)
        s = lax.dot_general(qs, kw, (((1,), (1,)), ((), ())),
                            preferred_element_type=F32)
        if valid_rows < tile:
            s = jnp.where(col < WINDOW + valid_rows, s, NEG_INF)
        if not has_cache and r0 < WINDOW:
            s = jnp.where(jnp.logical_or(col >= WINDOW - r0, t > 0), s, NEG_INF)
        m = jnp.maximum(jnp.max(s, axis=-1, keepdims=True), sink)
        p = jnp.exp(s - m)
        p = p / (jnp.sum(p, axis=-1, keepdims=True) + jnp.exp(sink - m))
        o = jnp.dot(p.astype(BF16), vw, preferred_element_type=F32)
        for jj in range(4):
            oj = jnp.where(low_half, o[jj * CHUNK:(jj + 1) * CHUNK, :],
                           o[(4 + jj) * CHUNK:(5 + jj) * CHUNK, :])
            mix_scr[r0:r0 + CHUNK, CONV_DIM + jj * KV_DIM:CONV_DIM + (jj + 1) * KV_DIM] = oj

    nk = k_scr[valid_rows:valid_rows + WINDOW, :]
    nv = v_scr[valid_rows:valid_rows + WINDOW, :]
    nk_ref[0] = nk
    nv_ref[0] = nv
    k_scr[0:WINDOW, :] = nk
    v_scr[0:WINDOW, :] = nv

    mix = jnp.dot(mix_scr[...].astype(BF16), wout_ref[...], preferred_element_type=F32)
    y_ref[0] = x + mix


def _even_layer(x, conv0, kc0, vc0, gain, win, convw, qg, kg, sink_rows, hsum_q, hsum_k, wout,
                *, tile, valid_rows, has_cache):
    b, s, _ = x.shape
    nt = s // tile
    const2 = lambda i, j: (0, 0)
    per_b = lambda i, j: (i, 0, 0)
    kern = functools.partial(_even_kernel, tile=tile, valid_rows=valid_rows, has_cache=has_cache)
    return pl.pallas_call(
        kern,
        grid=(b, nt),
        in_specs=[
            pl.BlockSpec((1, tile, D_MODEL), lambda i, j: (i, j, 0)),
            pl.BlockSpec((1, 2, CONV_DIM), per_b),
            pl.BlockSpec((1, WINDOW, KV_DIM), per_b),
            pl.BlockSpec((1, WINDOW, KV_DIM), per_b),
            pl.BlockSpec((1, D_MODEL), const2),
            pl.BlockSpec((D_MODEL, EVEN_IN_DIM), const2),
            pl.BlockSpec((3, CONV_DIM), const2),
            pl.BlockSpec((1, ATTN_DIM), const2),
            pl.BlockSpec((1, KV_DIM), const2),
            pl.BlockSpec((N_Q_HEADS * CHUNK, 1), const2),
            pl.BlockSpec((ATTN_DIM, ATTN_DIM), const2),
            pl.BlockSpec((KV_DIM, KV_DIM), const2),
            pl.BlockSpec((D_MODEL, D_MODEL), const2),
        ],
        out_specs=[
            pl.BlockSpec((1, tile, D_MODEL), lambda i, j: (i, j, 0)),
            pl.BlockSpec((1, 2, CONV_DIM), per_b),
            pl.BlockSpec((1, WINDOW, KV_DIM), per_b),
            pl.BlockSpec((1, WINDOW, KV_DIM), per_b),
        ],
        out_shape=[
            jax.ShapeDtypeStruct((b, s, D_MODEL), F32),
            jax.ShapeDtypeStruct((b, 2, CONV_DIM), F32),
            jax.ShapeDtypeStruct((b, WINDOW, KV_DIM), F32),
            jax.ShapeDtypeStruct((b, WINDOW, KV_DIM), F32),
        ],
        scratch_shapes=[
            pltpu.VMEM((8 + tile, CONV_DIM), F32),
            pltpu.VMEM((WINDOW + tile, KV_DIM), F32),
            pltpu.VMEM((WINDOW + tile, KV_DIM), F32),
            pltpu.VMEM((tile, D_MODEL), F32),
        ],
        compiler_params=pltpu.CompilerParams(
            dimension_semantics=("arbitrary", "arbitrary"),
            vmem_limit_bytes=VMEM_LIMIT_BYTES),
        name="even_layer",
    )(x, conv0, kc0, vc0, gain, win, convw, qg, kg, sink_rows, hsum_q, hsum_k, wout)


def _hgrn_kernel(x_ref, s0_ref, g_ref, win_ref, lb_ref, og_ref, wout_ref,
                 y_ref, snew_ref,
                 s_scr, q_scr, k_scr, v_scr, lf_scr, o_scr,
                 *, tile, valid_rows):
    t = pl.program_id(1)

    @pl.when(t == 0)
    def _():
        s_scr[...] = s0_ref[0]

    x = x_ref[0]
    xn = x * _rms_scale(x) * g_ref[...]
    z = jnp.dot(xn.astype(BF16), win_ref[...], preferred_element_type=F32)
    lb = lb_ref[...]
    fg = lb + (1.0 - lb) * jax.nn.sigmoid(z[:, D_MODEL:2 * D_MODEL])
    logf = jnp.log(fg)
    kk = 1.0 - fg
    if valid_rows < tile:
        row = lax.broadcasted_iota(jnp.int32, (tile, D_MODEL), 0)
        live = row < valid_rows
        logf = jnp.where(live, logf, 0.0)
        kk = jnp.where(live, kk, 0.0)
    q_scr[...] = z[:, 0:D_MODEL]
    k_scr[...] = kk
    v_scr[...] = z[:, 2 * D_MODEL:3 * D_MODEL]
    gate = z[:, 3 * D_MODEL:4 * D_MODEL]

    ri = lax.broadcasted_iota(jnp.int32, (tile, tile), 0)
    ci = lax.broadcasted_iota(jnp.int32, (tile, tile), 1)
    same = (ri // HGRN_CHUNK) == (ci // HGRN_CHUNK)
    tril = jnp.where(jnp.logical_and(same, ci <= ri), 1.0, 0.0).astype(BF16)
    hi = logf.astype(BF16)
    lo = (logf - hi.astype(F32)).astype(BF16)
    lf_scr[...] = (jnp.dot(tril, hi, preferred_element_type=F32)
                   + jnp.dot(tril, lo, preferred_element_type=F32))

    pair_blk = jnp.where(
        lax.broadcasted_iota(jnp.int32, (2 * HGRN_BLOCK, 2 * HGRN_DK), 0) // HGRN_BLOCK
        == lax.broadcasted_iota(jnp.int32, (2 * HGRN_BLOCK, 2 * HGRN_DK), 1) // HGRN_DK,
        1.0, 0.0).astype(BF16)

    for c in range(tile // HGRN_CHUNK):
        c0 = c * HGRN_CHUNK
        g = lf_scr[c0:c0 + HGRN_CHUNK, :]
        gtot = lf_scr[c0 + HGRN_CHUNK - 1:c0 + HGRN_CHUNK, :]
        qe = (q_scr[c0:c0 + HGRN_CHUNK, :] * jnp.exp(g)).astype(BF16)
        kh = k_scr[c0:c0 + HGRN_CHUNK, :] * jnp.exp(gtot - g)
        kh_t = kh.T.astype(BF16)
        dec_t = jnp.broadcast_to(jnp.exp(gtot), (HGRN_CHUNK, D_MODEL)).T
        vc = v_scr[c0:c0 + HGRN_CHUNK, :].astype(BF16)
        for h in range(HGRN_HEADS):
            hs = slice(h * HGRN_DK, (h + 1) * HGRN_DK)
            s_h = s_scr[h]
            o_scr[c0:c0 + HGRN_CHUNK, hs] = jnp.dot(
                qe[:, hs], s_h.astype(BF16), preferred_element_type=F32)
            s_scr[h] = dec_t[hs, :] * s_h + jnp.dot(
                kh_t[hs, :], vc[:, hs], preferred_element_type=F32)
        for j in range(HGRN_CHUNK // HGRN_BLOCK):
            r0 = c0 + j * HGRN_BLOCK
            nrow = HGRN_CHUNK - j * HGRN_BLOCK
            gj = lf_scr[r0:c0 + HGRN_CHUNK, :]
            if j == 0:
                rel = gj
            else:
                rel = gj - lf_scr[r0 - 1:r0, :]
            qj = (q_scr[r0:c0 + HGRN_CHUNK, :] * jnp.exp(rel)).astype(BF16)
            kj = (k_scr[r0:r0 + HGRN_BLOCK, :] * jnp.exp(-rel[0:HGRN_BLOCK, :])).astype(BF16)
            vj = v_scr[r0:r0 + HGRN_BLOCK, :].astype(BF16)
            causal = (lax.broadcasted_iota(jnp.int32, (nrow, 2 * HGRN_BLOCK), 0)
                      >= lax.broadcasted_iota(jnp.int32, (nrow, 2 * HGRN_BLOCK), 1) % HGRN_BLOCK)
            for hp in range(HGRN_HEADS // 2):
                ps = slice(2 * hp * HGRN_DK, (2 * hp + 2) * HGRN_DK)
                k2 = jnp.concatenate([kj[:, ps], kj[:, ps]], axis=0) * pair_blk
                v2 = jnp.concatenate([vj[:, ps], vj[:, ps]], axis=0) * pair_blk
                a = lax.dot_general(qj[:, ps], k2, (((1,), (1,)), ((), ())),
                                    preferred_element_type=F32)
                a = jnp.where(causal, a, 0.0).astype(BF16)
                o_scr[r0:c0 + HGRN_CHUNK, ps] += jnp.dot(a, v2, preferred_element_type=F32)

    snew_ref[0] = s_scr[...]
    o = o_scr[...]
    o = o * _rms_scale(o) * og_ref[...]
    o = o * (gate * jax.nn.sigmoid(gate))
    y_ref[0] = x + jnp.dot(o.astype(BF16), wout_ref[...], preferred_element_type=F32)


def _hgrn_layer(x, s0, gain, win, lb, out_gain, wout, *, tile, valid_rows):
    b, s, _ = x.shape
    nt = s // tile
    const2 = lambda i, j: (0, 0)
    kern = functools.partial(_hgrn_kernel, tile=tile, valid_rows=valid_rows)
    return pl.pallas_call(
        kern,
        grid=(b, nt),
        in_specs=[
            pl.BlockSpec((1, tile, D_MODEL), lambda i, j: (i, j, 0)),
            pl.BlockSpec((1, HGRN_HEADS, HGRN_DK, HGRN_DK), lambda i, j: (i, 0, 0, 0)),
            pl.BlockSpec((1, D_MODEL), const2),
            pl.BlockSpec((D_MODEL, 4 * D_MODEL), const2),
            pl.BlockSpec((1, D_MODEL), const2),
            pl.BlockSpec((1, D_MODEL), const2),
            pl.BlockSpec((D_MODEL, D_MODEL), const2),
        ],
        out_specs=[
            pl.BlockSpec((1, tile, D_MODEL), lambda i, j: (i, j, 0)),
            pl.BlockSpec((1, HGRN_HEADS, HGRN_DK, HGRN_DK), lambda i, j: (i, 0, 0, 0)),
        ],
        out_shape=[
            jax.ShapeDtypeStruct((b, s, D_MODEL), F32),
            jax.ShapeDtypeStruct((b, HGRN_HEADS, HGRN_DK, HGRN_DK), F32),
        ],
        scratch_shapes=[
            pltpu.VMEM((HGRN_HEADS, HGRN_DK, HGRN_DK), F32),
            pltpu.VMEM((tile, D_MODEL), F32),
            pltpu.VMEM((tile, D_MODEL), F32),
            pltpu.VMEM((tile, D_MODEL), F32),
            pltpu.VMEM((tile, D_MODEL), F32),
            pltpu.VMEM((tile, D_MODEL), F32),
        ],
        compiler_params=pltpu.CompilerParams(
            dimension_semantics=("arbitrary", "arbitrary"),
            vmem_limit_bytes=VMEM_LIMIT_BYTES),
        name="hgrn_layer",
    )(x, s0, gain, win, lb, out_gain, wout)


def _sort_network(n):
    pairs = []

    def merge(lo, hi, r):
        step = r * 2
        if step < hi - lo:
            merge(lo, hi, step)
            merge(lo + r, hi, step)
            pairs.extend((i, i + r) for i in range(lo + r, hi - r, step))
        else:
            pairs.append((lo, lo + r))

    def sort(lo, hi):
        if hi - lo >= 1:
            mid = lo + (hi - lo) // 2
            sort(lo, mid)
            sort(mid + 1, hi)
            merge(lo, hi, 1)

    sort(0, n - 1)
    return pairs


_SORT16 = _sort_network(PEER_TOPK)


def _sort16_desc(v):
    v = list(v)
    for i, j in _SORT16:
        v[i], v[j] = jnp.maximum(v[i], v[j]), jnp.minimum(v[i], v[j])
    return v


def _merge_top16(a, b):
    c = [jnp.maximum(a[i], b[PEER_TOPK - 1 - i]) for i in range(PEER_TOPK)]
    for d in (8, 4, 2, 1):
        for i in range(PEER_TOPK):
            if not i & d:
                c[i], c[i + d] = jnp.maximum(c[i], c[i + d]), jnp.minimum(c[i], c[i + d])
    return c


def _top16_sorted(load_slab):
    lists = [_sort16_desc([load_slab(PEER_TOPK * g + i) for i in range(PEER_TOPK)])
             for g in range(PEER_NKEYS // PEER_TOPK)]
    while len(lists) > 1:
        lists = [_merge_top16(a, b) for a, b in zip(lists[0::2], lists[1::2])]
    return lists[0]


def _bf16_pair_words(x):
    bits = lax.bitcast_convert_type(x.astype(BF16).astype(F32), jnp.int32)
    return bits | lax.shift_right_logical(bits, 16)


def _route_kernel(x_ref, g_ref, wq_ref, kb_ref,
                  xnt_ref, q2h_ref, e2h_ref, thr_ref, e1_ref,
                  s1_scr, s2_scr, a_scr, b_scr, cand_scr,
                  *, tile):
    x = x_ref[...]
    xn = x * _rms_scale(x) * g_ref[...]
    xnt = xn.T.astype(BF16)
    xnt_ref[...] = pltpu.bitcast(xnt, jnp.int32)
    qt = jnp.dot(wq_ref[...], xnt, preferred_element_type=F32).astype(BF16)
    half = PEER_HEADS * 128
    s1 = jnp.dot(kb_ref[0], qt[0:half, :], preferred_element_type=F32)
    s2 = jnp.dot(kb_ref[1], qt[half:2 * half, :], preferred_element_type=F32)
    s1_scr[...] = s1.reshape(PEER_NKEYS, PEER_HEADS, tile)
    s2_scr[...] = s2.reshape(PEER_NKEYS, PEER_HEADS, tile)
    for lt in range(tile // LANES):
        ls = slice(lt * LANES, (lt + 1) * LANES)
        for src, dst in ((s1_scr, a_scr), (s2_scr, b_scr)):
            top = _top16_sorted(lambda kidx, src=src: src[kidx, :, ls])
            for i in range(PEER_TOPK):
                dst[i, :, ls] = top[i]

    pairs = [(i, j) for i in range(PEER_TOPK) for j in range(PEER_TOPK)
             if (i + 1) * (j + 1) <= PEER_TOPK]
    rows = [[] for _ in range(PEER_TOPK)]
    for n, (i, j) in enumerate(pairs):
        c = a_scr[i] + b_scr[j]
        cand_scr[n] = c
        rows[i].append(c)
    s_a = _sort16_desc(rows[1] + [rows[i][0] for i in range(8, PEER_TOPK)])
    s_b = _sort16_desc(rows[2] + rows[3] + rows[4] + rows[5] + rows[6])
    top = _merge_top16(_merge_top16(rows[0], s_a), s_b)
    tau = jnp.minimum(top[13], jnp.minimum(jnp.maximum(top[14], rows[7][1]),
                                           jnp.maximum(top[15], rows[7][0])))

    a0 = a_scr[0]
    b0 = b_scr[0]
    zsum = jnp.zeros((PEER_HEADS, tile), F32)
    codes = []
    for i in range(PEER_TOPK):
        lam = jnp.zeros((PEER_HEADS, tile), F32)
        ea = jnp.exp(a_scr[i] - a0)
        for n, (pi, pj) in enumerate(pairs):
            if pi != i:
                continue
            sel = cand_scr[n] >= tau
            lam = lam + jnp.where(sel, 1.0, 0.0)
            zsum = zsum + jnp.where(sel, ea * jnp.exp(b_scr[pj] - b0), 0.0)
        codes.append((PEER_TOPK + 1.0) - lam)

    s1 = s1_scr[...]
    thr = jnp.full((PEER_NKEYS, PEER_HEADS, tile), PEER_TOPK + 1.0, F32)
    for i in range(PEER_TOPK):
        thr = jnp.where(s1 == a_scr[i][None], codes[i][None], thr)
    thr_w = _bf16_pair_words(thr).reshape(PEER_NKEYS * PEER_HEADS, tile)
    e1_w = _bf16_pair_words(jnp.exp(s1 - a0[None])).reshape(PEER_NKEYS * PEER_HEADS, tile)
    for lt in range(tile // LANES):
        thr_ref[lt] = thr_w[:, lt * LANES:(lt + 1) * LANES]
        e1_ref[lt] = e1_w[:, lt * LANES:(lt + 1) * LANES]

    s2hm = jnp.dot(kb_ref[2], qt[half:2 * half, :], preferred_element_type=F32)
    zscale = 0.5 / zsum
    for h in range(PEER_HEADS):
        s2h = s2hm[h * PEER_NKEYS:(h + 1) * PEER_NKEYS, :]
        q2 = jnp.zeros((PEER_NKEYS, tile), F32)
        for jj in range(PEER_TOPK - 1, -1, -1):
            q2 = jnp.where(s2h >= b_scr[jj, h:h + 1, :], float(PEER_TOPK - jj), q2)
        q2h_ref[h] = pltpu.bitcast(q2.astype(BF16), jnp.int32)
        e2 = jnp.where(q2 > 0.0, jnp.exp(s2h - b0[h:h + 1, :]) * zscale[h:h + 1, :], 0.0)
        e2h_ref[h] = pltpu.bitcast(e2.astype(BF16), jnp.int32)


def _peer_route(x, gain, wq_t, kbig, *, tile):
    n = x.shape[0]
    nt = n // tile
    npairs = sum(1 for i in range(PEER_TOPK) for j in range(PEER_TOPK)
                 if (i + 1) * (j + 1) <= PEER_TOPK)
    rows = PEER_NKEYS * PEER_HEADS
    kern = functools.partial(_route_kernel, tile=tile)
    return pl.pallas_call(
        kern,
        grid=(nt,),
        in_specs=[
            pl.BlockSpec((tile, D_MODEL), lambda i: (i, 0)),
            pl.BlockSpec((1, D_MODEL), lambda i: (0, 0)),
            pl.BlockSpec((2 * rows, D_MODEL), lambda i: (0, 0)),
            pl.BlockSpec((3, rows, rows), lambda i: (0, 0, 0)),
        ],
        out_specs=[
            pl.BlockSpec((D_MODEL // 2, tile), lambda i: (0, i)),
            pl.BlockSpec((PEER_HEADS, PEER_NKEYS // 2, tile), lambda i: (0, 0, i)),
            pl.BlockSpec((PEER_HEADS, PEER_NKEYS // 2, tile), lambda i: (0, 0, i)),
            pl.BlockSpec((tile // LANES, rows, LANES), lambda i: (i, 0, 0)),
            pl.BlockSpec((tile // LANES, rows, LANES), lambda i: (i, 0, 0)),
        ],
        out_shape=[
            jax.ShapeDtypeStruct((D_MODEL // 2, n), jnp.int32),
            jax.ShapeDtypeStruct((PEER_HEADS, PEER_NKEYS // 2, n), jnp.int32),
            jax.ShapeDtypeStruct((PEER_HEADS, PEER_NKEYS // 2, n), jnp.int32),
            jax.ShapeDtypeStruct((n // LANES, rows, LANES), jnp.int32),
            jax.ShapeDtypeStruct((n // LANES, rows, LANES), jnp.int32),
        ],
        scratch_shapes=[
            pltpu.VMEM((PEER_NKEYS, PEER_HEADS, tile), F32),
            pltpu.VMEM((PEER_NKEYS, PEER_HEADS, tile), F32),
            pltpu.VMEM((PEER_TOPK, PEER_HEADS, tile), F32),
            pltpu.VMEM((PEER_TOPK, PEER_HEADS, tile), F32),
            pltpu.VMEM((npairs, PEER_HEADS, tile), F32),
        ],
        compiler_params=pltpu.CompilerParams(
            dimension_semantics=("arbitrary",),
            vmem_limit_bytes=VMEM_LIMIT_BYTES),
        name="peer_route",
    )(x, gain, wq_t, kbig)


def _dense_kernel(x_ref, xnt_ref, q2h_ref, e2h_ref, thr_ref, e1_ref, u_ref, vt_ref,
                  y_ref, acc_ref, w_scr, *, tile, echunk):
    j = pl.program_id(1)

    @pl.when(j == 0)
    def _():
        acc_ref[...] = jnp.zeros_like(acc_ref)

    mxu_w = min(tile, MXU_DIM)
    ngrp = PEER_NKEYS // BF16_ROWS

    def bcast_row(word_ref, r, l0):
        row = word_ref[l0 // LANES, r:r + 1, :]
        return pltpu.bitcast(jnp.broadcast_to(row, (8, LANES)), BF16)

    h_all = jnp.dot(pltpu.bitcast(u_ref[0], BF16), pltpu.bitcast(xnt_ref[...], BF16),
                         preferred_element_type=F32)
    na = echunk // PEER_NKEYS
    ablk = 4
    gblk = 4
    for mb in range(tile // mxu_w):
        m0 = mb * mxu_w
        for lb in range(mxu_w // LANES):
            l0 = m0 + lb * LANES
            for gb in range(ngrp // gblk):
                grp = [gb * gblk + g for g in range(gblk)]
                for ab in range(na // ablk):
                    keys = [ab * ablk + a for a in range(ablk)]
                    gsum = [[jnp.zeros((BF16_ROWS, LANES), BF16) for _ in grp] for _ in keys]
                    for h in range(PEER_HEADS):
                        q2 = [pltpu.bitcast(q2h_ref[h, 8 * g:8 * g + 8, l0:l0 + LANES], BF16)
                              for g in grp]
                        e2 = [pltpu.bitcast(e2h_ref[h, 8 * g:8 * g + 8, l0:l0 + LANES], BF16)
                              for g in grp]
                        for ai, a in enumerate(keys):
                            thr = bcast_row(thr_ref, a * PEER_HEADS + h, l0)
                            e1 = bcast_row(e1_ref, a * PEER_HEADS + h, l0)
                            for g in range(gblk):
                                hit = q2[g] >= thr
                                gsum[ai][g] = gsum[ai][g] + jnp.where(hit, e2[g], 0.0) * e1
                    for ai, a in enumerate(keys):
                        for g in range(gblk):
                            e0 = a * PEER_NKEYS + grp[g] * BF16_ROWS
                            hv = h_all[e0:e0 + BF16_ROWS, l0:l0 + LANES]
                            act = (hv * (1.0 + lax.erf(hv * (2.0 ** -0.5)))).astype(BF16)
                            w_scr[e0:e0 + BF16_ROWS, l0:l0 + LANES] = (
                                jnp.where(gsum[ai][g] > 0.0, act, 0.0) * gsum[ai][g])
        acc_ref[:, m0:m0 + mxu_w] += jnp.dot(
            pltpu.bitcast(vt_ref[0], BF16), w_scr[:, m0:m0 + mxu_w],
            preferred_element_type=F32)

    @pl.when(j == pl.num_programs(1) - 1)
    def _():
        y_ref[...] = x_ref[...] + acc_ref[...].T


def _peer_dense(x, xnt, q2h, e2h, thr, e1, u_words, vt_words, *, layer, tile, echunk):
    n = x.shape[0]
    nt = n // tile
    nchunk = PEER_N_EXPERTS // echunk
    crows = (echunk // PEER_NKEYS) * PEER_HEADS
    kern = functools.partial(_dense_kernel, tile=tile, echunk=echunk)
    return pl.pallas_call(
        kern,
        grid=(nt, nchunk),
        in_specs=[
            pl.BlockSpec((tile, D_MODEL), lambda i, j: (i, 0)),
            pl.BlockSpec((D_MODEL // 2, tile), lambda i, j: (0, i)),
            pl.BlockSpec((PEER_HEADS, PEER_NKEYS // 2, tile), lambda i, j: (0, 0, i)),
            pl.BlockSpec((PEER_HEADS, PEER_NKEYS // 2, tile), lambda i, j: (0, 0, i)),
            pl.BlockSpec((tile // LANES, crows, LANES), lambda i, j: (i, j, 0)),
            pl.BlockSpec((tile // LANES, crows, LANES), lambda i, j: (i, j, 0)),
            pl.BlockSpec((1, echunk // 2, D_MODEL), lambda i, j: (layer, j, 0)),
            pl.BlockSpec((1, D_MODEL // 2, echunk), lambda i, j: (layer, 0, j)),
        ],
        out_specs=pl.BlockSpec((tile, D_MODEL), lambda i, j: (i, 0)),
        out_shape=jax.ShapeDtypeStruct((n, D_MODEL), F32),
        scratch_shapes=[pltpu.VMEM((D_MODEL, tile), F32),
                        pltpu.VMEM((echunk, tile), BF16)],
        compiler_params=pltpu.CompilerParams(
            dimension_semantics=("arbitrary", "arbitrary"),
            vmem_limit_bytes=VMEM_LIMIT_BYTES),
        name="peer_dense",
    )(x, xnt, q2h, e2h, thr, e1, u_words, vt_words)


def _peer(x, gain, wq_t, kbig, u_words, vt_words, *, layer, route_tile, dense_tile, echunk):
    xnt, q2h, e2h, thr, e1 = _peer_route(x, gain, wq_t, kbig, tile=route_tile)
    return _peer_dense(x, xnt, q2h, e2h, thr, e1, u_words, vt_words, layer=layer,
                       tile=dense_tile, echunk=echunk)


def _even_params(norm_g, w_in, conv_w, q_gain, k_gain, sinks, w_out):
    qcols = np.array([1536 + (j + 4 * hf) * HEAD_DIM + d
                      for j in range(4) for hf in range(2) for d in range(HEAD_DIM)])
    cols = np.concatenate([np.arange(1536), qcols, np.arange(2048, EVEN_IN_DIM)])
    orow = np.array([CONV_DIM + (j + 4 * hf) * HEAD_DIM + d
                     for j in range(4) for hf in range(2) for d in range(HEAD_DIM)])
    rows = np.concatenate([np.arange(CONV_DIM), orow])
    blk = lambda n: jnp.asarray(
        (np.arange(n)[:, None] // HEAD_DIM) == (np.arange(n)[None, :] // HEAD_DIM)).astype(BF16)
    return dict(
        gain=norm_g.reshape(1, D_MODEL),
        win=w_in[:, cols].astype(BF16),
        convw=conv_w,
        qg=jnp.tile(q_gain, N_Q_HEADS).reshape(1, ATTN_DIM),
        kg=jnp.tile(k_gain, N_KV_HEADS).reshape(1, KV_DIM),
        sink_rows=jnp.repeat(sinks, CHUNK).reshape(N_Q_HEADS * CHUNK, 1),
        hsum_q=blk(ATTN_DIM),
        hsum_k=blk(KV_DIM),
        wout=w_out[rows, :].astype(BF16),
    )


def _pack_tables_kernel(u_ref, v_ref, uo_ref, vo_ref):
    uo_ref[0] = pltpu.bitcast(u_ref[0].astype(BF16), jnp.int32)
    vo_ref[0] = pltpu.bitcast(v_ref[0].T.astype(BF16), jnp.int32)


def _pack_tables(u_tab, v_tab, *, eblk):
    nl, ne, d = u_tab.shape
    return pl.pallas_call(
        _pack_tables_kernel,
        grid=(nl, ne // eblk),
        in_specs=[pl.BlockSpec((1, eblk, d), lambda l, i: (l, i, 0)),
                  pl.BlockSpec((1, eblk, d), lambda l, i: (l, i, 0))],
        out_specs=[pl.BlockSpec((1, eblk // 2, d), lambda l, i: (l, i, 0)),
                   pl.BlockSpec((1, d // 2, eblk), lambda l, i: (l, 0, i))],
        out_shape=[jax.ShapeDtypeStruct((nl, ne // 2, d), jnp.int32),
                   jax.ShapeDtypeStruct((nl, d // 2, ne), jnp.int32)],
        compiler_params=pltpu.CompilerParams(
            dimension_semantics=("arbitrary", "arbitrary"),
            vmem_limit_bytes=VMEM_LIMIT_BYTES),
        name="pack_tables",
    )(u_tab, v_tab)


def _peer_params(norm_g, w_query, sub_keys):
    wq_t = w_query.T.reshape(PEER_HEADS, 2, 128, D_MODEL).transpose(1, 0, 2, 3)
    wq_t = wq_t.reshape(2 * PEER_HEADS * 128, D_MODEL).astype(BF16)
    eye = jnp.eye(PEER_HEADS, dtype=sub_keys.dtype)
    kbig = jnp.einsum('hpkd,hg->pkhgd', sub_keys, eye).reshape(
        2, PEER_NKEYS * PEER_HEADS, PEER_HEADS * 128)
    khm = jnp.einsum('hkd,hg->hkgd', sub_keys[:, 1], eye).reshape(
        1, PEER_HEADS * PEER_NKEYS, PEER_HEADS * 128)
    kbig = jnp.concatenate([kbig, khm], axis=0).astype(BF16)
    return dict(gain=norm_g.reshape(1, D_MODEL), wq_t=wq_t, kbig=kbig)


def kernel(x_prompt, x_sample, cache_conv, cache_k, cache_v, state_hgrn, norm_mix, norm_ffn,
           even_w_in, even_conv_w, even_q_gain, even_k_gain, even_sinks, even_w_out,
           hgrn_w_in, hgrn_lb, hgrn_out_gain, hgrn_w_out,
           peer_w_query, peer_sub_keys, peer_u, peer_v):
    bp, sp, _ = x_prompt.shape
    bs, ss, _ = x_sample.shape

    ev = _even_params(norm_mix[0], even_w_in[0], even_conv_w[0], even_q_gain[0],
                      even_k_gain[0], even_sinks[0], even_w_out[0])
    u_words, vt_words = _pack_tables(peer_u, peer_v, eblk=PEER_PACK_BLOCK)
    pe = [_peer_params(norm_ffn[l], peer_w_query[l], peer_sub_keys[l]) for l in range(2)]
    lbs = jax.nn.softmax(hgrn_lb.astype(F32), axis=0)
    lbs = jnp.cumsum(lbs, axis=0) - lbs[0]
    hg = dict(gain=norm_mix[1].reshape(1, D_MODEL), win=hgrn_w_in[0].astype(BF16),
              lb=lbs[1].reshape(1, D_MODEL), out_gain=hgrn_out_gain[0].reshape(1, D_MODEL),
              wout=hgrn_w_out[0].astype(BF16))

    def peer(x2d, l, route_tile, dense_tile):
        return _peer(x2d, pe[l]['gain'], pe[l]['wq_t'], pe[l]['kbig'], u_words, vt_words,
                     layer=l, route_tile=route_tile, dense_tile=dense_tile,
                     echunk=PEER_EXPERT_CHUNK)

    zc = jnp.zeros((bp, 2, CONV_DIM), F32)
    zkv = jnp.zeros((bp, WINDOW, KV_DIM), F32)
    rt = MIXER_ROW_TILE
    x, conv_p, k_p, v_p = _even_layer(x_prompt, zc, zkv, zkv, **ev,
                                      tile=rt, valid_rows=rt, has_cache=False)
    x = peer(x.reshape(bp * sp, D_MODEL), 0, PEER_ROUTE_TILE, PEER_DENSE_TILE)
    s0 = jnp.zeros((bp, HGRN_HEADS, HGRN_DK, HGRN_DK), F32)
    x, s_p = _hgrn_layer(x.reshape(bp, sp, D_MODEL), s0, **hg, tile=rt, valid_rows=rt)
    y_prompt = peer(x.reshape(bp * sp, D_MODEL), 1, PEER_ROUTE_TILE, PEER_DENSE_TILE)
    y_prompt = y_prompt.reshape(bp, sp, D_MODEL)

    ns = bs * ss
    xs = jnp.pad(x_sample, ((0, 0), (0, CHUNK - ss), (0, 0)))
    xs, conv_s, k_s, v_s = _even_layer(
        xs, cache_conv[0], cache_k[0].reshape(bs, WINDOW, KV_DIM),
        cache_v[0].reshape(bs, WINDOW, KV_DIM), **ev, tile=CHUNK, valid_rows=ss, has_cache=True)
    xs = peer(xs[:, :ss].reshape(ns, D_MODEL), 0, ns, ns).reshape(bs, ss, D_MODEL)
    xs = jnp.pad(xs, ((0, 0), (0, HGRN_CHUNK - ss), (0, 0)))
    xs, s_s = _hgrn_layer(xs, state_hgrn[0], **hg, tile=HGRN_CHUNK, valid_rows=ss)
    y_sample = peer(xs[:, :ss].reshape(ns, D_MODEL), 1, ns, ns).reshape(bs, ss, D_MODEL)

    kv5 = lambda a, b: a.reshape(1, b, WINDOW, N_KV_HEADS, HEAD_DIM)
    return (y_prompt, y_sample, conv_p[None], kv5(k_p, bp), kv5(v_p, bp), s_p[None],
            conv_s[None], kv5(k_s, bs), kv5(v_s, bs), s_s[None])
```

```python
import functools

import jax
import jax.numpy as jnp
import numpy as np
from jax import lax
from jax.experimental import pallas as pl
from jax.experimental.pallas import tpu as pltpu

F32 = jnp.float32
BF16 = jnp.bfloat16

D_MODEL = 1024
RMS_EPS = 1e-6
CHUNK = 64
WINDOW = 128
CONV_DIM = 512
N_Q_HEADS = 8
N_KV_HEADS = 2
HEAD_DIM = 64
ATTN_DIM = 512
KV_DIM = 128
EVEN_IN_DIM = 2304
HGRN_HEADS = 8
HGRN_DK = 128
HGRN_BLOCK = 16
HGRN_CHUNK = 128
PEER_HEADS = 8
PEER_NKEYS = 128
PEER_TOPK = 16
PEER_N_EXPERTS = PEER_NKEYS * PEER_NKEYS
LANES = 128
BF16_ROWS = 16
MXU_DIM = 256

MIXER_ROW_TILE = 256
PEER_ROUTE_TILE = 512
PEER_DENSE_TILE = 1024
PEER_EXPERT_CHUNK = 1024
PEER_PACK_BLOCK = 1024

VMEM_LIMIT_BYTES = 52 * 1024 * 1024

NEG_INF = float("-inf")


def _rms_scale(x):
    return lax.rsqrt(jnp.mean(x * x, axis=-1, keepdims=True) + RMS_EPS)


def _split_dot(x, w_bf16):
    hi = x.astype(BF16)
    lo = (x - hi.astype(F32)).astype(BF16)
    return (jnp.dot(hi, w_bf16, preferred_element_type=F32)
            + jnp.dot(lo, w_bf16, preferred_element_type=F32))


def _even_kernel(x_ref, conv0_ref, kc0_ref, vc0_ref, g_ref, win_ref, convw_ref,
                 qg_ref, kg_ref, sink_ref, hsum_q_ref, hsum_k_ref, wout_ref,
                 y_ref, nconv_ref, nk_ref, nv_ref,
                 u_scr, k_scr, v_scr, mix_scr,
                 *, tile, valid_rows, has_cache):
    t = pl.program_id(1)

    @pl.when(t == 0)
    def _():
        u_scr[0:8, :] = jnp.zeros((8, CONV_DIM), F32)
        u_scr[6:8, :] = conv0_ref[0]
        k_scr[0:WINDOW, :] = kc0_ref[0]
        v_scr[0:WINDOW, :] = vc0_ref[0]

    x = x_ref[0]
    xn = x * _rms_scale(x) * g_ref[...]
    z = jnp.dot(xn.astype(BF16), win_ref[...], preferred_element_type=F32)
    bg = z[:, 0:512]
    cg = z[:, 512:1024]
    hh = z[:, 1024:1536]
    q = z[:, 1536:2048]
    k = z[:, 2048:2176]
    v = z[:, 2176:2304]

    u = cg * hh
    u_scr[8:8 + tile, :] = u
    cw = convw_ref[...]
    conv = (cw[0:1, :] * u_scr[6:6 + tile, :] + cw[1:2, :] * u_scr[7:7 + tile, :]
            + cw[2:3, :] * u)
    mix_scr[:, 0:CONV_DIM] = bg * conv
    tail = u_scr[6 + valid_rows:8 + valid_rows, :]
    nconv_ref[0] = tail
    u_scr[6:8, :] = tail

    q_ms = _split_dot(q * q, hsum_q_ref[...]) * (1.0 / HEAD_DIM)
    q = q * lax.rsqrt(q_ms + RMS_EPS) * qg_ref[...] * (HEAD_DIM ** -0.5)
    k_ms = _split_dot(k * k, hsum_k_ref[...]) * (1.0 / HEAD_DIM)
    k = k * lax.rsqrt(k_ms + RMS_EPS) * kg_ref[...]
    k_scr[WINDOW:WINDOW + tile, :] = k
    v_scr[WINDOW:WINDOW + tile, :] = v

    lane = lax.broadcasted_iota(jnp.int32, (CHUNK, KV_DIM), 1)
    low_half = lane < HEAD_DIM
    sink = sink_ref[...]
    nkeys = WINDOW + CHUNK
    col = lax.broadcasted_iota(jnp.int32, (N_Q_HEADS * CHUNK, nkeys), 1)
    for j in range(tile // CHUNK):
        r0 = j * CHUNK
        blocks = []
        for b in range(N_Q_HEADS):
            qv = q[r0:r0 + CHUNK, (b % 4) * KV_DIM:(b % 4 + 1) * KV_DIM]
            keep = low_half if b < 4 else jnp.logical_not(low_half)
            blocks.append(jnp.where(keep, qv, 0.0))
        qs = jnp.concatenate(blocks, axis=0).astype(BF16)
        kw = k_scr[r0:r0 + nkeys, :].astype(BF16)
        vw = v_scr[r0:r0 + nkeys, :].astype(BF16)
        s = lax.dot_general(qs, kw, (((1,), (1,)), ((), ())),
                            preferred_element_type=F32)
        if valid_rows < tile:
            s = jnp.where(col < WINDOW + valid_rows, s, NEG_INF)
        if not has_cache and r0 < WINDOW:
            s = jnp.where(jnp.logical_or(col >= WINDOW - r0, t > 0), s, NEG_INF)
        m = jnp.maximum(jnp.max(s, axis=-1, keepdims=True), sink)
        p = jnp.exp(s - m)
        p = p / (jnp.sum(p, axis=-1, keepdims=True) + jnp.exp(sink - m))
        o = jnp.dot(p.astype(BF16), vw, preferred_element_type=F32)
        for jj in range(4):
            oj = jnp.where(low_half, o[jj * CHUNK:(jj + 1) * CHUNK, :],
                           o[(4 + jj) * CHUNK:(5 + jj) * CHUNK, :])
            mix_scr[r0:r0 + CHUNK, CONV_DIM + jj * KV_DIM:CONV_DIM + (jj + 1) * KV_DIM] = oj

    nk = k_scr[valid_rows:valid_rows + WINDOW, :]
    nv = v_scr[valid_rows:valid_rows + WINDOW, :]
    nk_ref[0] = nk
    nv_ref[0] = nv
    k_scr[0:WINDOW, :] = nk
    v_scr[0:WINDOW, :] = nv

    mix = jnp.dot(mix_scr[...].astype(BF16), wout_ref[...], preferred_element_type=F32)
    y_ref[0] = x + mix


def _even_layer(x, conv0, kc0, vc0, gain, win, convw, qg, kg, sink_rows, hsum_q, hsum_k, wout,
                *, tile, valid_rows, has_cache):
    b, s, _ = x.shape
    nt = s // tile
    const2 = lambda i, j: (0, 0)
    per_b = lambda i, j: (i, 0, 0)
    kern = functools.partial(_even_kernel, tile=tile, valid_rows=valid_rows, has_cache=has_cache)
    return pl.pallas_call(
        kern,
        grid=(b, nt),
        in_specs=[
            pl.BlockSpec((1, tile, D_MODEL), lambda i, j: (i, j, 0)),
            pl.BlockSpec((1, 2, CONV_DIM), per_b),
            pl.BlockSpec((1, WINDOW, KV_DIM), per_b),
            pl.BlockSpec((1, WINDOW, KV_DIM), per_b),
            pl.BlockSpec((1, D_MODEL), const2),
            pl.BlockSpec((D_MODEL, EVEN_IN_DIM), const2),
            pl.BlockSpec((3, CONV_DIM), const2),
            pl.BlockSpec((1, ATTN_DIM), const2),
            pl.BlockSpec((1, KV_DIM), const2),
            pl.BlockSpec((N_Q_HEADS * CHUNK, 1), const2),
            pl.BlockSpec((ATTN_DIM, ATTN_DIM), const2),
            pl.BlockSpec((KV_DIM, KV_DIM), const2),
            pl.BlockSpec((D_MODEL, D_MODEL), const2),
        ],
        out_specs=[
            pl.BlockSpec((1, tile, D_MODEL), lambda i, j: (i, j, 0)),
            pl.BlockSpec((1, 2, CONV_DIM), per_b),
            pl.BlockSpec((1, WINDOW, KV_DIM), per_b),
            pl.BlockSpec((1, WINDOW, KV_DIM), per_b),
        ],
        out_shape=[
            jax.ShapeDtypeStruct((b, s, D_MODEL), F32),
            jax.ShapeDtypeStruct((b, 2, CONV_DIM), F32),
            jax.ShapeDtypeStruct((b, WINDOW, KV_DIM), F32),
            jax.ShapeDtypeStruct((b, WINDOW, KV_DIM), F32),
        ],
        scratch_shapes=[
            pltpu.VMEM((8 + tile, CONV_DIM), F32),
            pltpu.VMEM((WINDOW + tile, KV_DIM), F32),
            pltpu.VMEM((WINDOW + tile, KV_DIM), F32),
            pltpu.VMEM((tile, D_MODEL), F32),
        ],
        compiler_params=pltpu.CompilerParams(
            dimension_semantics=("arbitrary", "arbitrary"),
            vmem_limit_bytes=VMEM_LIMIT_BYTES),
        name="even_layer",
    )(x, conv0, kc0, vc0, gain, win, convw, qg, kg, sink_rows, hsum_q, hsum_k, wout)


def _hgrn_kernel(x_ref, s0_ref, g_ref, win_ref, lb_ref, og_ref, wout_ref,
                 y_ref, snew_ref,
                 s_scr, q_scr, k_scr, v_scr, lf_scr, o_scr,
                 *, tile, valid_rows):
    t = pl.program_id(1)

    @pl.when(t == 0)
    def _():
        s_scr[...] = s0_ref[0]

    x = x_ref[0]
    xn = x * _rms_scale(x) * g_ref[...]
    z = jnp.dot(xn.astype(BF16), win_ref[...], preferred_element_type=F32)
    lb = lb_ref[...]
    fg = lb + (1.0 - lb) * jax.nn.sigmoid(z[:, D_MODEL:2 * D_MODEL])
    logf = jnp.log(fg)
    kk = 1.0 - fg
    if valid_rows < tile:
        row = lax.broadcasted_iota(jnp.int32, (tile, D_MODEL), 0)
        live = row < valid_rows
        logf = jnp.where(live, logf, 0.0)
        kk = jnp.where(live, kk, 0.0)
    q_scr[...] = z[:, 0:D_MODEL]
    k_scr[...] = kk
    v_scr[...] = z[:, 2 * D_MODEL:3 * D_MODEL]
    gate = z[:, 3 * D_MODEL:4 * D_MODEL]

    ri = lax.broadcasted_iota(jnp.int32, (tile, tile), 0)
    ci = lax.broadcasted_iota(jnp.int32, (tile, tile), 1)
    same = (ri // HGRN_CHUNK) == (ci // HGRN_CHUNK)
    tril = jnp.where(jnp.logical_and(same, ci <= ri), 1.0, 0.0).astype(BF16)
    hi = logf.astype(BF16)
    lo = (logf - hi.astype(F32)).astype(BF16)
    lf_scr[...] = (jnp.dot(tril, hi, preferred_element_type=F32)
                   + jnp.dot(tril, lo, preferred_element_type=F32))

    pair_blk = jnp.where(
        lax.broadcasted_iota(jnp.int32, (2 * HGRN_BLOCK, 2 * HGRN_DK), 0) // HGRN_BLOCK
        == lax.broadcasted_iota(jnp.int32, (2 * HGRN_BLOCK, 2 * HGRN_DK), 1) // HGRN_DK,
        1.0, 0.0).astype(BF16)

    for c in range(tile // HGRN_CHUNK):
        c0 = c * HGRN_CHUNK
        g = lf_scr[c0:c0 + HGRN_CHUNK, :]
        gtot = lf_scr[c0 + HGRN_CHUNK - 1:c0 + HGRN_CHUNK, :]
        qe = (q_scr[c0:c0 + HGRN_CHUNK, :] * jnp.exp(g)).astype(BF16)
        kh = k_scr[c0:c0 + HGRN_CHUNK, :] * jnp.exp(gtot - g)
        kh_t = kh.T.astype(BF16)
        dec_t = jnp.broadcast_to(jnp.exp(gtot), (HGRN_CHUNK, D_MODEL)).T
        vc = v_scr[c0:c0 + HGRN_CHUNK, :].astype(BF16)
        for h in range(HGRN_HEADS):
            hs = slice(h * HGRN_DK, (h + 1) * HGRN_DK)
            s_h = s_scr[h]
            o_scr[c0:c0 + HGRN_CHUNK, hs] = jnp.dot(
                qe[:, hs], s_h.astype(BF16), preferred_element_type=F32)
            s_scr[h] = dec_t[hs, :] * s_h + jnp.dot(
                kh_t[hs, :], vc[:, hs], preferred_element_type=F32)
        for j in range(HGRN_CHUNK // HGRN_BLOCK):
            r0 = c0 + j * HGRN_BLOCK
            nrow = HGRN_CHUNK - j * HGRN_BLOCK
            gj = lf_scr[r0:c0 + HGRN_CHUNK, :]
            if j == 0:
                rel = gj
            else:
                rel = gj - lf_scr[r0 - 1:r0, :]
            qj = (q_scr[r0:c0 + HGRN_CHUNK, :] * jnp.exp(rel)).astype(BF16)
            kj = (k_scr[r0:r0 + HGRN_BLOCK, :] * jnp.exp(-rel[0:HGRN_BLOCK, :])).astype(BF16)
            vj = v_scr[r0:r0 + HGRN_BLOCK, :].astype(BF16)
            causal = (lax.broadcasted_iota(jnp.int32, (nrow, 2 * HGRN_BLOCK), 0)
                      >= lax.broadcasted_iota(jnp.int32, (nrow, 2 * HGRN_BLOCK), 1) % HGRN_BLOCK)
            for hp in range(HGRN_HEADS // 2):
                ps = slice(2 * hp * HGRN_DK, (2 * hp + 2) * HGRN_DK)
                k2 = jnp.concatenate([kj[:, ps], kj[:, ps]], axis=0) * pair_blk
                v2 = jnp.concatenate([vj[:, ps], vj[:, ps]], axis=0) * pair_blk
                a = lax.dot_general(qj[:, ps], k2, (((1,), (1,)), ((), ())),
                                    preferred_element_type=F32)
                a = jnp.where(causal, a, 0.0).astype(BF16)
                o_scr[r0:c0 + HGRN_CHUNK, ps] += jnp.dot(a, v2, preferred_element_type=F32)

    snew_ref[0] = s_scr[...]
    o = o_scr[...]
    o = o * _rms_scale(o) * og_ref[...]
    o = o * (gate * jax.nn.sigmoid(gate))
    y_ref[0] = x + jnp.dot(o.astype(BF16), wout_ref[...], preferred_element_type=F32)


def _hgrn_layer(x, s0, gain, win, lb, out_gain, wout, *, tile, valid_rows):
    b, s, _ = x.shape
    nt = s // tile
    const2 = lambda i, j: (0, 0)
    kern = functools.partial(_hgrn_kernel, tile=tile, valid_rows=valid_rows)
    return pl.pallas_call(
        kern,
        grid=(b, nt),
        in_specs=[
            pl.BlockSpec((1, tile, D_MODEL), lambda i, j: (i, j, 0)),
            pl.BlockSpec((1, HGRN_HEADS, HGRN_DK, HGRN_DK), lambda i, j: (i, 0, 0, 0)),
            pl.BlockSpec((1, D_MODEL), const2),
            pl.BlockSpec((D_MODEL, 4 * D_MODEL), const2),
            pl.BlockSpec((1, D_MODEL), const2),
            pl.BlockSpec((1, D_MODEL), const2),
            pl.BlockSpec((D_MODEL, D_MODEL), const2),
        ],
        out_specs=[
            pl.BlockSpec((1, tile, D_MODEL), lambda i, j: (i, j, 0)),
            pl.BlockSpec((1, HGRN_HEADS, HGRN_DK, HGRN_DK), lambda i, j: (i, 0, 0, 0)),
        ],
        out_shape=[
            jax.ShapeDtypeStruct((b, s, D_MODEL), F32),
            jax.ShapeDtypeStruct((b, HGRN_HEADS, HGRN_DK, HGRN_DK), F32),
        ],
        scratch_shapes=[
            pltpu.VMEM((HGRN_HEADS, HGRN_DK, HGRN_DK), F32),
            pltpu.VMEM((tile, D_MODEL), F32),
            pltpu.VMEM((tile, D_MODEL), F32),
            pltpu.VMEM((tile, D_MODEL), F32),
            pltpu.VMEM((tile, D_MODEL), F32),
            pltpu.VMEM((tile, D_MODEL), F32),
        ],
        compiler_params=pltpu.CompilerParams(
            dimension_semantics=("arbitrary", "arbitrary"),
            vmem_limit_bytes=VMEM_LIMIT_BYTES),
        name="hgrn_layer",
    )(x, s0, gain, win, lb, out_gain, wout)


def _sort_network(n):
    pairs = []

    def merge(lo, hi, r):
        step = r * 2
        if step < hi - lo:
            merge(lo, hi, step)
            merge(lo + r, hi, step)
            pairs.extend((i, i + r) for i in range(lo + r, hi - r, step))
        else:
            pairs.append((lo, lo + r))

    def sort(lo, hi):
        if hi - lo >= 1:
            mid = lo + (hi - lo) // 2
            sort(lo, mid)
            sort(mid + 1, hi)
            merge(lo, hi, 1)

    sort(0, n - 1)
    return pairs


_SORT16 = _sort_network(PEER_TOPK)


def _sort16_desc(v):
    v = list(v)
    for i, j in _SORT16:
        v[i], v[j] = jnp.maximum(v[i], v[j]), jnp.minimum(v[i], v[j])
    return v


def _merge_top16(a, b):
    c = [jnp.maximum(a[i], b[PEER_TOPK - 1 - i]) for i in range(PEER_TOPK)]
    for d in (8, 4, 2, 1):
        for i in range(PEER_TOPK):
            if not i & d:
                c[i], c[i + d] = jnp.maximum(c[i], c[i + d]), jnp.minimum(c[i], c[i + d])
    return c


def _top16_sorted(load_slab):
    lists = [_sort16_desc([load_slab(PEER_TOPK * g + i) for i in range(PEER_TOPK)])
             for g in range(PEER_NKEYS // PEER_TOPK)]
    while len(lists) > 1:
        lists = [_merge_top16(a, b) for a, b in zip(lists[0::2], lists[1::2])]
    return lists[0]


def _bf16_pair_words(x):
    bits = lax.bitcast_convert_type(x.astype(BF16).astype(F32), jnp.int32)
    return bits | lax.shift_right_logical(bits, 16)


def _route_kernel(x_ref, g_ref, wq_ref, kb_ref,
                  xnt_ref, q2h_ref, e2h_ref, thr_ref, e1_ref,
                  s1_scr, s2_scr, a_scr, b_scr, cand_scr,
                  *, tile):
    x = x_ref[...]
    xn = x * _rms_scale(x) * g_ref[...]
    xnt = xn.T.astype(BF16)
    xnt_ref[...] = pltpu.bitcast(xnt, jnp.int32)
    qt = jnp.dot(wq_ref[...], xnt, preferred_element_type=F32).astype(BF16)
    half = PEER_HEADS * 128
    s1 = jnp.dot(kb_ref[0], qt[0:half, :], preferred_element_type=F32)
    s2 = jnp.dot(kb_ref[1], qt[half:2 * half, :], preferred_element_type=F32)
    s1_scr[...] = s1.reshape(PEER_NKEYS, PEER_HEADS, tile)
    s2_scr[...] = s2.reshape(PEER_NKEYS, PEER_HEADS, tile)
    for lt in range(tile // LANES):
        ls = slice(lt * LANES, (lt + 1) * LANES)
        for src, dst in ((s1_scr, a_scr), (s2_scr, b_scr)):
            top = _top16_sorted(lambda kidx, src=src: src[kidx, :, ls])
            for i in range(PEER_TOPK):
                dst[i, :, ls] = top[i]

    pairs = [(i, j) for i in range(PEER_TOPK) for j in range(PEER_TOPK)
             if (i + 1) * (j + 1) <= PEER_TOPK]
    rows = [[] for _ in range(PEER_TOPK)]
    for n, (i, j) in enumerate(pairs):
        c = a_scr[i] + b_scr[j]
        cand_scr[n] = c
        rows[i].append(c)
    s_a = _sort16_desc(rows[1] + [rows[i][0] for i in range(8, PEER_TOPK)])
    s_b = _sort16_desc(rows[2] + rows[3] + rows[4] + rows[5] + rows[6])
    top = _merge_top16(_merge_top16(rows[0], s_a), s_b)
    tau = jnp.minimum(top[13], jnp.minimum(jnp.maximum(top[14], rows[7][1]),
                                           jnp.maximum(top[15], rows[7][0])))

    a0 = a_scr[0]
    b0 = b_scr[0]
    zsum = jnp.zeros((PEER_HEADS, tile), F32)
    codes = []
    for i in range(PEER_TOPK):
        lam = jnp.zeros((PEER_HEADS, tile), F32)
        ea = jnp.exp(a_scr[i] - a0)
        for n, (pi, pj) in enumerate(pairs):
            if pi != i:
                continue
            sel = cand_scr[n] >= tau
            lam = lam + jnp.where(sel, 1.0, 0.0)
            zsum = zsum + jnp.where(sel, ea * jnp.exp(b_scr[pj] - b0), 0.0)
        codes.append((PEER_TOPK + 1.0) - lam)

    s1 = s1_scr[...]
    thr = jnp.full((PEER_NKEYS, PEER_HEADS, tile), PEER_TOPK + 1.0, F32)
    for i in range(PEER_TOPK):
        thr = jnp.where(s1 == a_scr[i][None], codes[i][None], thr)
    thr_w = _bf16_pair_words(thr).reshape(PEER_NKEYS * PEER_HEADS, tile)
    e1_w = _bf16_pair_words(jnp.exp(s1 - a0[None])).reshape(PEER_NKEYS * PEER_HEADS, tile)
    for lt in range(tile // LANES):
        thr_ref[lt] = thr_w[:, lt * LANES:(lt + 1) * LANES]
        e1_ref[lt] = e1_w[:, lt * LANES:(lt + 1) * LANES]

    s2hm = jnp.dot(kb_ref[2], qt[half:2 * half, :], preferred_element_type=F32)
    zscale = 0.5 / zsum
    for h in range(PEER_HEADS):
        s2h = s2hm[h * PEER_NKEYS:(h + 1) * PEER_NKEYS, :]
        q2 = jnp.zeros((PEER_NKEYS, tile), F32)
        for jj in range(PEER_TOPK - 1, -1, -1):
            q2 = jnp.where(s2h >= b_scr[jj, h:h + 1, :], float(PEER_TOPK - jj), q2)
        q2h_ref[h] = pltpu.bitcast(q2.astype(BF16), jnp.int32)
        e2 = jnp.where(q2 > 0.0, jnp.exp(s2h - b0[h:h + 1, :]) * zscale[h:h + 1, :], 0.0)
        e2h_ref[h] = pltpu.bitcast(e2.astype(BF16), jnp.int32)


def _peer_route(x, gain, wq_t, kbig, *, tile):
    n = x.shape[0]
    nt = n // tile
    npairs = sum(1 for i in range(PEER_TOPK) for j in range(PEER_TOPK)
                 if (i + 1) * (j + 1) <= PEER_TOPK)
    rows = PEER_NKEYS * PEER_HEADS
    kern = functools.partial(_route_kernel, tile=tile)
    return pl.pallas_call(
        kern,
        grid=(nt,),
        in_specs=[
            pl.BlockSpec((tile, D_MODEL), lambda i: (i, 0)),
            pl.BlockSpec((1, D_MODEL), lambda i: (0, 0)),
            pl.BlockSpec((2 * rows, D_MODEL), lambda i: (0, 0)),
            pl.BlockSpec((3, rows, rows), lambda i: (0, 0, 0)),
        ],
        out_specs=[
            pl.BlockSpec((D_MODEL // 2, tile), lambda i: (0, i)),
            pl.BlockSpec((PEER_HEADS, PEER_NKEYS // 2, tile), lambda i: (0, 0, i)),
            pl.BlockSpec((PEER_HEADS, PEER_NKEYS // 2, tile), lambda i: (0, 0, i)),
            pl.BlockSpec((tile // LANES, rows, LANES), lambda i: (i, 0, 0)),
            pl.BlockSpec((tile // LANES, rows, LANES), lambda i: (i, 0, 0)),
        ],
        out_shape=[
            jax.ShapeDtypeStruct((D_MODEL // 2, n), jnp.int32),
            jax.ShapeDtypeStruct((PEER_HEADS, PEER_NKEYS // 2, n), jnp.int32),
            jax.ShapeDtypeStruct((PEER_HEADS, PEER_NKEYS // 2, n), jnp.int32),
            jax.ShapeDtypeStruct((n // LANES, rows, LANES), jnp.int32),
            jax.ShapeDtypeStruct((n // LANES, rows, LANES), jnp.int32),
        ],
        scratch_shapes=[
            pltpu.VMEM((PEER_NKEYS, PEER_HEADS, tile), F32),
            pltpu.VMEM((PEER_NKEYS, PEER_HEADS, tile), F32),
            pltpu.VMEM((PEER_TOPK, PEER_HEADS, tile), F32),
            pltpu.VMEM((PEER_TOPK, PEER_HEADS, tile), F32),
            pltpu.VMEM((npairs, PEER_HEADS, tile), F32),
        ],
        compiler_params=pltpu.CompilerParams(
            dimension_semantics=("arbitrary",),
            vmem_limit_bytes=VMEM_LIMIT_BYTES),
        name="peer_route",
    )(x, gain, wq_t, kbig)


def _dense_kernel(x_ref, xnt_ref, q2h_ref, e2h_ref, thr_ref, e1_ref, u_ref, vt_ref,
                  y_ref, acc_ref, w_scr, *, tile, echunk):
    j = pl.program_id(1)

    @pl.when(j == 0)
    def _():
        acc_ref[...] = jnp.zeros_like(acc_ref)

    mxu_w = min(tile, MXU_DIM)
    ngrp = PEER_NKEYS // BF16_ROWS

    def bcast_row(word_ref, r, l0):
        row = word_ref[l0 // LANES, r:r + 1, :]
        return pltpu.bitcast(jnp.broadcast_to(row, (8, LANES)), BF16)

    h_all = jnp.dot(pltpu.bitcast(u_ref[0], BF16), pltpu.bitcast(xnt_ref[...], BF16),
                         preferred_element_type=F32)
    na = echunk // PEER_NKEYS
    ablk = 4
    gblk = 4
    for mb in range(tile // mxu_w):
        m0 = mb * mxu_w
        for lb in range(mxu_w // LANES):
            l0 = m0 + lb * LANES
            for gb in range(ngrp // gblk):
                grp = [gb * gblk + g for g in range(gblk)]
                for ab in range(na // ablk):
                    keys = [ab * ablk + a for a in range(ablk)]
                    gsum = [[jnp.zeros((BF16_ROWS, LANES), BF16) for _ in grp] for _ in keys]
                    for h in range(PEER_HEADS):
                        q2 = [pltpu.bitcast(q2h_ref[h, 8 * g:8 * g + 8, l0:l0 + LANES], BF16)
                              for g in grp]
                        e2 = [pltpu.bitcast(e2h_ref[h, 8 * g:8 * g + 8, l0:l0 + LANES], BF16)
                              for g in grp]
                        for ai, a in enumerate(keys):
                            thr = bcast_row(thr_ref, a * PEER_HEADS + h, l0)
                            e1 = bcast_row(e1_ref, a * PEER_HEADS + h, l0)
                            for g in range(gblk):
                                hit = q2[g] >= thr
                                gsum[ai][g] = gsum[ai][g] + jnp.where(hit, e2[g], 0.0) * e1
                    for ai, a in enumerate(keys):
                        for g in range(gblk):
                            e0 = a * PEER_NKEYS + grp[g] * BF16_ROWS
                            hv = h_all[e0:e0 + BF16_ROWS, l0:l0 + LANES]
                            act = (hv * (1.0 + lax.erf(hv * (2.0 ** -0.5)))).astype(BF16)
                            w_scr[e0:e0 + BF16_ROWS, l0:l0 + LANES] = (
                                jnp.where(gsum[ai][g] > 0.0, act, 0.0) * gsum[ai][g])
    acc_ref[...] += jnp.dot(pltpu.bitcast(vt_ref[0], BF16), w_scr[...],
                            preferred_element_type=F32)

    @pl.when(j == pl.num_programs(1) - 1)
    def _():
        y_ref[...] = x_ref[...] + acc_ref[...].T


def _peer_dense(x, xnt, q2h, e2h, thr, e1, u_words, vt_words, *, layer, tile, echunk):
    n = x.shape[0]
    nt = n // tile
    nchunk = PEER_N_EXPERTS // echunk
    crows = (echunk // PEER_NKEYS) * PEER_HEADS
    kern = functools.partial(_dense_kernel, tile=tile, echunk=echunk)
    return pl.pallas_call(
        kern,
        grid=(nt, nchunk),
        in_specs=[
            pl.BlockSpec((tile, D_MODEL), lambda i, j: (i, 0)),
            pl.BlockSpec((D_MODEL // 2, tile), lambda i, j: (0, i)),
            pl.BlockSpec((PEER_HEADS, PEER_NKEYS // 2, tile), lambda i, j: (0, 0, i)),
            pl.BlockSpec((PEER_HEADS, PEER_NKEYS // 2, tile), lambda i, j: (0, 0, i)),
            pl.BlockSpec((tile // LANES, crows, LANES), lambda i, j: (i, j, 0)),
            pl.BlockSpec((tile // LANES, crows, LANES), lambda i, j: (i, j, 0)),
            pl.BlockSpec((1, echunk // 2, D_MODEL), lambda i, j: (layer, j, 0)),
            pl.BlockSpec((1, D_MODEL // 2, echunk), lambda i, j: (layer, 0, j)),
        ],
        out_specs=pl.BlockSpec((tile, D_MODEL), lambda i, j: (i, 0)),
        out_shape=jax.ShapeDtypeStruct((n, D_MODEL), F32),
        scratch_shapes=[pltpu.VMEM((D_MODEL, tile), F32),
                        pltpu.VMEM((echunk, tile), BF16)],
        compiler_params=pltpu.CompilerParams(
            dimension_semantics=("arbitrary", "arbitrary"),
            vmem_limit_bytes=VMEM_LIMIT_BYTES),
        name="peer_dense",
    )(x, xnt, q2h, e2h, thr, e1, u_words, vt_words)


def _peer(x, gain, wq_t, kbig, u_words, vt_words, *, layer, route_tile, dense_tile, echunk):
    xnt, q2h, e2h, thr, e1 = _peer_route(x, gain, wq_t, kbig, tile=route_tile)
    return _peer_dense(x, xnt, q2h, e2h, thr, e1, u_words, vt_words, layer=layer,
                       tile=dense_tile, echunk=echunk)


def _even_params(norm_g, w_in, conv_w, q_gain, k_gain, sinks, w_out):
    qcols = np.array([1536 + (j + 4 * hf) * HEAD_DIM + d
                      for j in range(4) for hf in range(2) for d in range(HEAD_DIM)])
    cols = np.concatenate([np.arange(1536), qcols, np.arange(2048, EVEN_IN_DIM)])
    orow = np.array([CONV_DIM + (j + 4 * hf) * HEAD_DIM + d
                     for j in range(4) for hf in range(2) for d in range(HEAD_DIM)])
    rows = np.concatenate([np.arange(CONV_DIM), orow])
    blk = lambda n: jnp.asarray(
        (np.arange(n)[:, None] // HEAD_DIM) == (np.arange(n)[None, :] // HEAD_DIM)).astype(BF16)
    return dict(
        gain=norm_g.reshape(1, D_MODEL),
        win=w_in[:, cols].astype(BF16),
        convw=conv_w,
        qg=jnp.tile(q_gain, N_Q_HEADS).reshape(1, ATTN_DIM),
        kg=jnp.tile(k_gain, N_KV_HEADS).reshape(1, KV_DIM),
        sink_rows=jnp.repeat(sinks, CHUNK).reshape(N_Q_HEADS * CHUNK, 1),
        hsum_q=blk(ATTN_DIM),
        hsum_k=blk(KV_DIM),
        wout=w_out[rows, :].astype(BF16),
    )


def _pack_tables_kernel(u_ref, v_ref, uo_ref, vo_ref):
    uo_ref[0] = pltpu.bitcast(u_ref[0].astype(BF16), jnp.int32)
    vo_ref[0] = pltpu.bitcast(v_ref[0].T.astype(BF16), jnp.int32)


def _pack_tables(u_tab, v_tab, *, eblk):
    nl, ne, d = u_tab.shape
    return pl.pallas_call(
        _pack_tables_kernel,
        grid=(nl, ne // eblk),
        in_specs=[pl.BlockSpec((1, eblk, d), lambda l, i: (l, i, 0)),
                  pl.BlockSpec((1, eblk, d), lambda l, i: (l, i, 0))],
        out_specs=[pl.BlockSpec((1, eblk // 2, d), lambda l, i: (l, i, 0)),
                   pl.BlockSpec((1, d // 2, eblk), lambda l, i: (l, 0, i))],
        out_shape=[jax.ShapeDtypeStruct((nl, ne // 2, d), jnp.int32),
                   jax.ShapeDtypeStruct((nl, d // 2, ne), jnp.int32)],
        compiler_params=pltpu.CompilerParams(
            dimension_semantics=("arbitrary", "arbitrary"),
            vmem_limit_bytes=VMEM_LIMIT_BYTES),
        name="pack_tables",
    )(u_tab, v_tab)


def _peer_params(norm_g, w_query, sub_keys):
    wq_t = w_query.T.reshape(PEER_HEADS, 2, 128, D_MODEL).transpose(1, 0, 2, 3)
    wq_t = wq_t.reshape(2 * PEER_HEADS * 128, D_MODEL).astype(BF16)
    eye = jnp.eye(PEER_HEADS, dtype=sub_keys.dtype)
    kbig = jnp.einsum('hpkd,hg->pkhgd', sub_keys, eye).reshape(
        2, PEER_NKEYS * PEER_HEADS, PEER_HEADS * 128)
    khm = jnp.einsum('hkd,hg->hkgd', sub_keys[:, 1], eye).reshape(
        1, PEER_HEADS * PEER_NKEYS, PEER_HEADS * 128)
    kbig = jnp.concatenate([kbig, khm], axis=0).astype(BF16)
    return dict(gain=norm_g.reshape(1, D_MODEL), wq_t=wq_t, kbig=kbig)


def kernel(x_prompt, x_sample, cache_conv, cache_k, cache_v, state_hgrn, norm_mix, norm_ffn,
           even_w_in, even_conv_w, even_q_gain, even_k_gain, even_sinks, even_w_out,
           hgrn_w_in, hgrn_lb, hgrn_out_gain, hgrn_w_out,
           peer_w_query, peer_sub_keys, peer_u, peer_v):
    bp, sp, _ = x_prompt.shape
    bs, ss, _ = x_sample.shape

    ev = _even_params(norm_mix[0], even_w_in[0], even_conv_w[0], even_q_gain[0],
                      even_k_gain[0], even_sinks[0], even_w_out[0])
    u_words, vt_words = _pack_tables(peer_u, peer_v, eblk=PEER_PACK_BLOCK)
    pe = [_peer_params(norm_ffn[l], peer_w_query[l], peer_sub_keys[l]) for l in range(2)]
    lbs = jax.nn.softmax(hgrn_lb.astype(F32), axis=0)
    lbs = jnp.cumsum(lbs, axis=0) - lbs[0]
    hg = dict(gain=norm_mix[1].reshape(1, D_MODEL), win=hgrn_w_in[0].astype(BF16),
              lb=lbs[1].reshape(1, D_MODEL), out_gain=hgrn_out_gain[0].reshape(1, D_MODEL),
              wout=hgrn_w_out[0].astype(BF16))

    def peer(x2d, l, route_tile, dense_tile):
        return _peer(x2d, pe[l]['gain'], pe[l]['wq_t'], pe[l]['kbig'], u_words, vt_words,
                     layer=l, route_tile=route_tile, dense_tile=dense_tile,
                     echunk=PEER_EXPERT_CHUNK)

    zc = jnp.zeros((bp, 2, CONV_DIM), F32)
    zkv = jnp.zeros((bp, WINDOW, KV_DIM), F32)
    rt = MIXER_ROW_TILE
    x, conv_p, k_p, v_p = _even_layer(x_prompt, zc, zkv, zkv, **ev,
                                      tile=rt, valid_rows=rt, has_cache=False)
    x = peer(x.reshape(bp * sp, D_MODEL), 0, PEER_ROUTE_TILE, PEER_DENSE_TILE)
    s0 = jnp.zeros((bp, HGRN_HEADS, HGRN_DK, HGRN_DK), F32)
    x, s_p = _hgrn_layer(x.reshape(bp, sp, D_MODEL), s0, **hg, tile=rt, valid_rows=rt)
    y_prompt = peer(x.reshape(bp * sp, D_MODEL), 1, PEER_ROUTE_TILE, PEER_DENSE_TILE)
    y_prompt = y_prompt.reshape(bp, sp, D_MODEL)

    ns = bs * ss
    xs = jnp.pad(x_sample, ((0, 0), (0, CHUNK - ss), (0, 0)))
    xs, conv_s, k_s, v_s = _even_layer(
        xs, cache_conv[0], cache_k[0].reshape(bs, WINDOW, KV_DIM),
        cache_v[0].reshape(bs, WINDOW, KV_DIM), **ev, tile=CHUNK, valid_rows=ss, has_cache=True)
    xs = peer(xs[:, :ss].reshape(ns, D_MODEL), 0, ns, ns).reshape(bs, ss, D_MODEL)
    xs = jnp.pad(xs, ((0, 0), (0, HGRN_CHUNK - ss), (0, 0)))
    xs, s_s = _hgrn_layer(xs, state_hgrn[0], **hg, tile=HGRN_CHUNK, valid_rows=ss)
    y_sample = peer(xs[:, :ss].reshape(ns, D_MODEL), 1, ns, ns).reshape(bs, ss, D_MODEL)

    kv5 = lambda a, b: a.reshape(1, b, WINDOW, N_KV_HEADS, HEAD_DIM)
    return (y_prompt, y_sample, conv_p[None], kv5(k_p, bp), kv5(v_p, bp), s_p[None],
            conv_s[None], kv5(k_s, bs), kv5(v_s, bs), s_s[None])
```

```python
import functools

import jax
import jax.numpy as jnp
import numpy as np
from jax import lax
from jax.experimental import pallas as pl
from jax.experimental.pallas import tpu as pltpu

F32 = jnp.float32
BF16 = jnp.bfloat16

D_MODEL = 1024
RMS_EPS = 1e-6
CHUNK = 64
WINDOW = 128
CONV_DIM = 512
N_Q_HEADS = 8
N_KV_HEADS = 2
HEAD_DIM = 64
ATTN_DIM = 512
KV_DIM = 128
EVEN_IN_DIM = 2304
HGRN_HEADS = 8
HGRN_DK = 128
HGRN_BLOCK = 16
HGRN_CHUNK = 128
PEER_HEADS = 8
PEER_NKEYS = 128
PEER_TOPK = 16
PEER_N_EXPERTS = PEER_NKEYS * PEER_NKEYS
LANES = 128
BF16_ROWS = 16
MXU_DIM = 256

MIXER_ROW_TILE = 256
PEER_ROUTE_TILE = 512
PEER_DENSE_TILE = 1024
PEER_EXPERT_CHUNK = 1024
PEER_PACK_BLOCK = 1024
TABLE_SLOTS = 3

VMEM_LIMIT_BYTES = 52 * 1024 * 1024

NEG_INF = float("-inf")


def _rms_scale(x):
    return lax.rsqrt(jnp.mean(x * x, axis=-1, keepdims=True) + RMS_EPS)


def _split_dot(x, w_bf16):
    hi = x.astype(BF16)
    lo = (x - hi.astype(F32)).astype(BF16)
    return (jnp.dot(hi, w_bf16, preferred_element_type=F32)
            + jnp.dot(lo, w_bf16, preferred_element_type=F32))


def _even_kernel(x_ref, conv0_ref, kc0_ref, vc0_ref, g_ref, win_ref, convw_ref,
                 qg_ref, kg_ref, sink_ref, hsum_q_ref, hsum_k_ref, wout_ref,
                 y_ref, nconv_ref, nk_ref, nv_ref,
                 u_scr, k_scr, v_scr, mix_scr,
                 *, tile, valid_rows, has_cache):
    t = pl.program_id(1)

    @pl.when(t == 0)
    def _():
        u_scr[0:8, :] = jnp.zeros((8, CONV_DIM), F32)
        u_scr[6:8, :] = conv0_ref[0]
        k_scr[0:WINDOW, :] = kc0_ref[0]
        v_scr[0:WINDOW, :] = vc0_ref[0]

    x = x_ref[0]
    xn = x * _rms_scale(x) * g_ref[...]
    z = jnp.dot(xn.astype(BF16), win_ref[...], preferred_element_type=F32)
    bg = z[:, 0:512]
    cg = z[:, 512:1024]
    hh = z[:, 1024:1536]
    q = z[:, 1536:2048]
    k = z[:, 2048:2176]
    v = z[:, 2176:2304]

    u = cg * hh
    u_scr[8:8 + tile, :] = u
    cw = convw_ref[...]
    conv = (cw[0:1, :] * u_scr[6:6 + tile, :] + cw[1:2, :] * u_scr[7:7 + tile, :]
            + cw[2:3, :] * u)
    mix_scr[:, 0:CONV_DIM] = bg * conv
    tail = u_scr[6 + valid_rows:8 + valid_rows, :]
    nconv_ref[0] = tail
    u_scr[6:8, :] = tail

    q_ms = _split_dot(q * q, hsum_q_ref[...]) * (1.0 / HEAD_DIM)
    q = q * lax.rsqrt(q_ms + RMS_EPS) * qg_ref[...] * (HEAD_DIM ** -0.5)
    k_ms = _split_dot(k * k, hsum_k_ref[...]) * (1.0 / HEAD_DIM)
    k = k * lax.rsqrt(k_ms + RMS_EPS) * kg_ref[...]
    k_scr[WINDOW:WINDOW + tile, :] = k
    v_scr[WINDOW:WINDOW + tile, :] = v

    lane = lax.broadcasted_iota(jnp.int32, (CHUNK, KV_DIM), 1)
    low_half = lane < HEAD_DIM
    sink = sink_ref[...]
    nkeys = WINDOW + CHUNK
    col = lax.broadcasted_iota(jnp.int32, (N_Q_HEADS * CHUNK, nkeys), 1)
    for j in range(tile // CHUNK):
        r0 = j * CHUNK
        blocks = []
        for b in range(N_Q_HEADS):
            qv = q[r0:r0 + CHUNK, (b % 4) * KV_DIM:(b % 4 + 1) * KV_DIM]
            keep = low_half if b < 4 else jnp.logical_not(low_half)
            blocks.append(jnp.where(keep, qv, 0.0))
        qs = jnp.concatenate(blocks, axis=0).astype(BF16)
        kw = k_scr[r0:r0 + nkeys, :].astype(BF16)
        vw = v_scr[r0:r0 + nkeys, :].astype(BF16)
        s = lax.dot_general(qs, kw, (((1,), (1,)), ((), ())),
                            preferred_element_type=F32)
        if valid_rows < tile:
            s = jnp.where(col < WINDOW + valid_rows, s, NEG_INF)
        if not has_cache and r0 < WINDOW:
            s = jnp.where(jnp.logical_or(col >= WINDOW - r0, t > 0), s, NEG_INF)
        m = jnp.maximum(jnp.max(s, axis=-1, keepdims=True), sink)
        p = jnp.exp(s - m)
        p = p / (jnp.sum(p, axis=-1, keepdims=True) + jnp.exp(sink - m))
        o = jnp.dot(p.astype(BF16), vw, preferred_element_type=F32)
        for jj in range(4):
            oj = jnp.where(low_half, o[jj * CHUNK:(jj + 1) * CHUNK, :],
                           o[(4 + jj) * CHUNK:(5 + jj) * CHUNK, :])
            mix_scr[r0:r0 + CHUNK, CONV_DIM + jj * KV_DIM:CONV_DIM + (jj + 1) * KV_DIM] = oj

    nk = k_scr[valid_rows:valid_rows + WINDOW, :]
    nv = v_scr[valid_rows:valid_rows + WINDOW, :]
    nk_ref[0] = nk
    nv_ref[0] = nv
    k_scr[0:WINDOW, :] = nk
    v_scr[0:WINDOW, :] = nv

    mix = jnp.dot(mix_scr[...].astype(BF16), wout_ref[...], preferred_element_type=F32)
    y_ref[0] = x + mix


def _even_layer(x, conv0, kc0, vc0, gain, win, convw, qg, kg, sink_rows, hsum_q, hsum_k, wout,
                *, tile, valid_rows, has_cache):
    b, s, _ = x.shape
    nt = s // tile
    const2 = lambda i, j: (0, 0)
    per_b = lambda i, j: (i, 0, 0)
    kern = functools.partial(_even_kernel, tile=tile, valid_rows=valid_rows, has_cache=has_cache)
    return pl.pallas_call(
        kern,
        grid=(b, nt),
        in_specs=[
            pl.BlockSpec((1, tile, D_MODEL), lambda i, j: (i, j, 0)),
            pl.BlockSpec((1, 2, CONV_DIM), per_b),
            pl.BlockSpec((1, WINDOW, KV_DIM), per_b),
            pl.BlockSpec((1, WINDOW, KV_DIM), per_b),
            pl.BlockSpec((1, D_MODEL), const2),
            pl.BlockSpec((D_MODEL, EVEN_IN_DIM), const2),
            pl.BlockSpec((3, CONV_DIM), const2),
            pl.BlockSpec((1, ATTN_DIM), const2),
            pl.BlockSpec((1, KV_DIM), const2),
            pl.BlockSpec((N_Q_HEADS * CHUNK, 1), const2),
            pl.BlockSpec((ATTN_DIM, ATTN_DIM), const2),
            pl.BlockSpec((KV_DIM, KV_DIM), const2),
            pl.BlockSpec((D_MODEL, D_MODEL), const2),
        ],
        out_specs=[
            pl.BlockSpec((1, tile, D_MODEL), lambda i, j: (i, j, 0)),
            pl.BlockSpec((1, 2, CONV_DIM), per_b),
            pl.BlockSpec((1, WINDOW, KV_DIM), per_b),
            pl.BlockSpec((1, WINDOW, KV_DIM), per_b),
        ],
        out_shape=[
            jax.ShapeDtypeStruct((b, s, D_MODEL), F32),
            jax.ShapeDtypeStruct((b, 2, CONV_DIM), F32),
            jax.ShapeDtypeStruct((b, WINDOW, KV_DIM), F32),
            jax.ShapeDtypeStruct((b, WINDOW, KV_DIM), F32),
        ],
        scratch_shapes=[
            pltpu.VMEM((8 + tile, CONV_DIM), F32),
            pltpu.VMEM((WINDOW + tile, KV_DIM), F32),
            pltpu.VMEM((WINDOW + tile, KV_DIM), F32),
            pltpu.VMEM((tile, D_MODEL), F32),
        ],
        compiler_params=pltpu.CompilerParams(
            dimension_semantics=("arbitrary", "arbitrary"),
            vmem_limit_bytes=VMEM_LIMIT_BYTES),
        name="even_layer",
    )(x, conv0, kc0, vc0, gain, win, convw, qg, kg, sink_rows, hsum_q, hsum_k, wout)


def _hgrn_kernel(x_ref, s0_ref, g_ref, win_ref, lb_ref, og_ref, wout_ref,
                 y_ref, snew_ref,
                 s_scr, q_scr, k_scr, v_scr, lf_scr, o_scr,
                 *, tile, valid_rows):
    t = pl.program_id(1)

    @pl.when(t == 0)
    def _():
        s_scr[...] = s0_ref[0]

    x = x_ref[0]
    xn = x * _rms_scale(x) * g_ref[...]
    z = jnp.dot(xn.astype(BF16), win_ref[...], preferred_element_type=F32)
    lb = lb_ref[...]
    fg = lb + (1.0 - lb) * jax.nn.sigmoid(z[:, D_MODEL:2 * D_MODEL])
    logf = jnp.log(fg)
    kk = 1.0 - fg
    if valid_rows < tile:
        row = lax.broadcasted_iota(jnp.int32, (tile, D_MODEL), 0)
        live = row < valid_rows
        logf = jnp.where(live, logf, 0.0)
        kk = jnp.where(live, kk, 0.0)
    q_scr[...] = z[:, 0:D_MODEL]
    k_scr[...] = kk
    v_scr[...] = z[:, 2 * D_MODEL:3 * D_MODEL]
    gate = z[:, 3 * D_MODEL:4 * D_MODEL]

    ri = lax.broadcasted_iota(jnp.int32, (tile, tile), 0)
    ci = lax.broadcasted_iota(jnp.int32, (tile, tile), 1)
    same = (ri // HGRN_CHUNK) == (ci // HGRN_CHUNK)
    tril = jnp.where(jnp.logical_and(same, ci <= ri), 1.0, 0.0).astype(BF16)
    hi = logf.astype(BF16)
    lo = (logf - hi.astype(F32)).astype(BF16)
    lf_scr[...] = (jnp.dot(tril, hi, preferred_element_type=F32)
                   + jnp.dot(tril, lo, preferred_element_type=F32))

    pair_blk = jnp.where(
        lax.broadcasted_iota(jnp.int32, (2 * HGRN_BLOCK, 2 * HGRN_DK), 0) // HGRN_BLOCK
        == lax.broadcasted_iota(jnp.int32, (2 * HGRN_BLOCK, 2 * HGRN_DK), 1) // HGRN_DK,
        1.0, 0.0).astype(BF16)

    for c in range(tile // HGRN_CHUNK):
        c0 = c * HGRN_CHUNK
        g = lf_scr[c0:c0 + HGRN_CHUNK, :]
        gtot = lf_scr[c0 + HGRN_CHUNK - 1:c0 + HGRN_CHUNK, :]
        qe = (q_scr[c0:c0 + HGRN_CHUNK, :] * jnp.exp(g)).astype(BF16)
        kh = k_scr[c0:c0 + HGRN_CHUNK, :] * jnp.exp(gtot - g)
        kh_t = kh.T.astype(BF16)
        dec_t = jnp.broadcast_to(jnp.exp(gtot), (HGRN_CHUNK, D_MODEL)).T
        vc = v_scr[c0:c0 + HGRN_CHUNK, :].astype(BF16)
        for h in range(HGRN_HEADS):
            hs = slice(h * HGRN_DK, (h + 1) * HGRN_DK)
            s_h = s_scr[h]
            o_scr[c0:c0 + HGRN_CHUNK, hs] = jnp.dot(
                qe[:, hs], s_h.astype(BF16), preferred_element_type=F32)
            s_scr[h] = dec_t[hs, :] * s_h + jnp.dot(
                kh_t[hs, :], vc[:, hs], preferred_element_type=F32)
        for j in range(HGRN_CHUNK // HGRN_BLOCK):
            r0 = c0 + j * HGRN_BLOCK
            nrow = HGRN_CHUNK - j * HGRN_BLOCK
            gj = lf_scr[r0:c0 + HGRN_CHUNK, :]
            if j == 0:
                rel = gj
            else:
                rel = gj - lf_scr[r0 - 1:r0, :]
            qj = (q_scr[r0:c0 + HGRN_CHUNK, :] * jnp.exp(rel)).astype(BF16)
            kj = (k_scr[r0:r0 + HGRN_BLOCK, :] * jnp.exp(-rel[0:HGRN_BLOCK, :])).astype(BF16)
            vj = v_scr[r0:r0 + HGRN_BLOCK, :].astype(BF16)
            causal = (lax.broadcasted_iota(jnp.int32, (nrow, 2 * HGRN_BLOCK), 0)
                      >= lax.broadcasted_iota(jnp.int32, (nrow, 2 * HGRN_BLOCK), 1) % HGRN_BLOCK)
            for hp in range(HGRN_HEADS // 2):
                ps = slice(2 * hp * HGRN_DK, (2 * hp + 2) * HGRN_DK)
                k2 = jnp.concatenate([kj[:, ps], kj[:, ps]], axis=0) * pair_blk
                v2 = jnp.concatenate([vj[:, ps], vj[:, ps]], axis=0) * pair_blk
                a = lax.dot_general(qj[:, ps], k2, (((1,), (1,)), ((), ())),
                                    preferred_element_type=F32)
                a = jnp.where(causal, a, 0.0).astype(BF16)
                o_scr[r0:c0 + HGRN_CHUNK, ps] += jnp.dot(a, v2, preferred_element_type=F32)

    snew_ref[0] = s_scr[...]
    o = o_scr[...]
    o = o * _rms_scale(o) * og_ref[...]
    o = o * (gate * jax.nn.sigmoid(gate))
    y_ref[0] = x + jnp.dot(o.astype(BF16), wout_ref[...], preferred_element_type=F32)


def _hgrn_layer(x, s0, gain, win, lb, out_gain, wout, *, tile, valid_rows):
    b, s, _ = x.shape
    nt = s // tile
    const2 = lambda i, j: (0, 0)
    kern = functools.partial(_hgrn_kernel, tile=tile, valid_rows=valid_rows)
    return pl.pallas_call(
        kern,
        grid=(b, nt),
        in_specs=[
            pl.BlockSpec((1, tile, D_MODEL), lambda i, j: (i, j, 0)),
            pl.BlockSpec((1, HGRN_HEADS, HGRN_DK, HGRN_DK), lambda i, j: (i, 0, 0, 0)),
            pl.BlockSpec((1, D_MODEL), const2),
            pl.BlockSpec((D_MODEL, 4 * D_MODEL), const2),
            pl.BlockSpec((1, D_MODEL), const2),
            pl.BlockSpec((1, D_MODEL), const2),
            pl.BlockSpec((D_MODEL, D_MODEL), const2),
        ],
        out_specs=[
            pl.BlockSpec((1, tile, D_MODEL), lambda i, j: (i, j, 0)),
            pl.BlockSpec((1, HGRN_HEADS, HGRN_DK, HGRN_DK), lambda i, j: (i, 0, 0, 0)),
        ],
        out_shape=[
            jax.ShapeDtypeStruct((b, s, D_MODEL), F32),
            jax.ShapeDtypeStruct((b, HGRN_HEADS, HGRN_DK, HGRN_DK), F32),
        ],
        scratch_shapes=[
            pltpu.VMEM((HGRN_HEADS, HGRN_DK, HGRN_DK), F32),
            pltpu.VMEM((tile, D_MODEL), F32),
            pltpu.VMEM((tile, D_MODEL), F32),
            pltpu.VMEM((tile, D_MODEL), F32),
            pltpu.VMEM((tile, D_MODEL), F32),
            pltpu.VMEM((tile, D_MODEL), F32),
        ],
        compiler_params=pltpu.CompilerParams(
            dimension_semantics=("arbitrary", "arbitrary"),
            vmem_limit_bytes=VMEM_LIMIT_BYTES),
        name="hgrn_layer",
    )(x, s0, gain, win, lb, out_gain, wout)


def _sort_network(n):
    pairs = []

    def merge(lo, hi, r):
        step = r * 2
        if step < hi - lo:
            merge(lo, hi, step)
            merge(lo + r, hi, step)
            pairs.extend((i, i + r) for i in range(lo + r, hi - r, step))
        else:
            pairs.append((lo, lo + r))

    def sort(lo, hi):
        if hi - lo >= 1:
            mid = lo + (hi - lo) // 2
            sort(lo, mid)
            sort(mid + 1, hi)
            merge(lo, hi, 1)

    sort(0, n - 1)
    return pairs


_SORT16 = _sort_network(PEER_TOPK)


def _sort16_desc(v):
    v = list(v)
    for i, j in _SORT16:
        v[i], v[j] = jnp.maximum(v[i], v[j]), jnp.minimum(v[i], v[j])
    return v


def _merge_top16(a, b):
    c = [jnp.maximum(a[i], b[PEER_TOPK - 1 - i]) for i in range(PEER_TOPK)]
    for d in (8, 4, 2, 1):
        for i in range(PEER_TOPK):
            if not i & d:
                c[i], c[i + d] = jnp.maximum(c[i], c[i + d]), jnp.minimum(c[i], c[i + d])
    return c


def _top16_sorted(load_slab):
    lists = [_sort16_desc([load_slab(PEER_TOPK * g + i) for i in range(PEER_TOPK)])
             for g in range(PEER_NKEYS // PEER_TOPK)]
    while len(lists) > 1:
        lists = [_merge_top16(a, b) for a, b in zip(lists[0::2], lists[1::2])]
    return lists[0]


def _bf16_pair_words(x):
    bits = lax.bitcast_convert_type(x.astype(BF16).astype(F32), jnp.int32)
    return bits | lax.shift_right_logical(bits, 16)


def _route_kernel(x_ref, g_ref, wq_ref, kb_ref,
                  xnt_ref, q2h_ref, e2h_ref, thr_ref, e1_ref,
                  s1_scr, s2_scr, a_scr, b_scr, cand_scr,
                  *, tile):
    x = x_ref[...]
    xn = x * _rms_scale(x) * g_ref[...]
    xnt = xn.T.astype(BF16)
    xnt_ref[...] = pltpu.bitcast(xnt, jnp.int32)
    qt = jnp.dot(wq_ref[...], xnt, preferred_element_type=F32).astype(BF16)
    half = PEER_HEADS * 128
    s1 = jnp.dot(kb_ref[0], qt[0:half, :], preferred_element_type=F32)
    s2 = jnp.dot(kb_ref[1], qt[half:2 * half, :], preferred_element_type=F32)
    s1_scr[...] = s1.reshape(PEER_NKEYS, PEER_HEADS, tile)
    s2_scr[...] = s2.reshape(PEER_NKEYS, PEER_HEADS, tile)
    for lt in range(tile // LANES):
        ls = slice(lt * LANES, (lt + 1) * LANES)
        for src, dst in ((s1_scr, a_scr), (s2_scr, b_scr)):
            top = _top16_sorted(lambda kidx, src=src: src[kidx, :, ls])
            for i in range(PEER_TOPK):
                dst[i, :, ls] = top[i]

    pairs = [(i, j) for i in range(PEER_TOPK) for j in range(PEER_TOPK)
             if (i + 1) * (j + 1) <= PEER_TOPK]
    rows = [[] for _ in range(PEER_TOPK)]
    for n, (i, j) in enumerate(pairs):
        c = a_scr[i] + b_scr[j]
        cand_scr[n] = c
        rows[i].append(c)
    s_a = _sort16_desc(rows[1] + [rows[i][0] for i in range(8, PEER_TOPK)])
    s_b = _sort16_desc(rows[2] + rows[3] + rows[4] + rows[5] + rows[6])
    top = _merge_top16(_merge_top16(rows[0], s_a), s_b)
    tau = jnp.minimum(top[13], jnp.minimum(jnp.maximum(top[14], rows[7][1]),
                                           jnp.maximum(top[15], rows[7][0])))

    a0 = a_scr[0]
    b0 = b_scr[0]
    zsum = jnp.zeros((PEER_HEADS, tile), F32)
    codes = []
    for i in range(PEER_TOPK):
        lam = jnp.zeros((PEER_HEADS, tile), F32)
        ea = jnp.exp(a_scr[i] - a0)
        for n, (pi, pj) in enumerate(pairs):
            if pi != i:
                continue
            sel = cand_scr[n] >= tau
            lam = lam + jnp.where(sel, 1.0, 0.0)
            zsum = zsum + jnp.where(sel, ea * jnp.exp(b_scr[pj] - b0), 0.0)
        codes.append((PEER_TOPK + 1.0) - lam)

    s1 = s1_scr[...]
    thr = jnp.full((PEER_NKEYS, PEER_HEADS, tile), PEER_TOPK + 1.0, F32)
    for i in range(PEER_TOPK):
        thr = jnp.where(s1 == a_scr[i][None], codes[i][None], thr)
    thr_w = _bf16_pair_words(thr).reshape(PEER_NKEYS * PEER_HEADS, tile)
    e1_w = _bf16_pair_words(jnp.exp(s1 - a0[None])).reshape(PEER_NKEYS * PEER_HEADS, tile)
    for lt in range(tile // LANES):
        thr_ref[lt] = thr_w[:, lt * LANES:(lt + 1) * LANES]
        e1_ref[lt] = e1_w[:, lt * LANES:(lt + 1) * LANES]

    s2hm = jnp.dot(kb_ref[2], qt[half:2 * half, :], preferred_element_type=F32)
    zscale = 0.5 / zsum
    for h in range(PEER_HEADS):
        s2h = s2hm[h * PEER_NKEYS:(h + 1) * PEER_NKEYS, :]
        q2 = jnp.zeros((PEER_NKEYS, tile), F32)
        for jj in range(PEER_TOPK - 1, -1, -1):
            q2 = jnp.where(s2h >= b_scr[jj, h:h + 1, :], float(PEER_TOPK - jj), q2)
        q2h_ref[h] = pltpu.bitcast(q2.astype(BF16), jnp.int32)
        e2 = jnp.where(q2 > 0.0, jnp.exp(s2h - b0[h:h + 1, :]) * zscale[h:h + 1, :], 0.0)
        e2h_ref[h] = pltpu.bitcast(e2.astype(BF16), jnp.int32)


def _peer_route(x, gain, wq_t, kbig, *, tile):
    n = x.shape[0]
    nt = n // tile
    npairs = sum(1 for i in range(PEER_TOPK) for j in range(PEER_TOPK)
                 if (i + 1) * (j + 1) <= PEER_TOPK)
    rows = PEER_NKEYS * PEER_HEADS
    kern = functools.partial(_route_kernel, tile=tile)
    return pl.pallas_call(
        kern,
        grid=(nt,),
        in_specs=[
            pl.BlockSpec((tile, D_MODEL), lambda i: (i, 0)),
            pl.BlockSpec((1, D_MODEL), lambda i: (0, 0)),
            pl.BlockSpec((2 * rows, D_MODEL), lambda i: (0, 0)),
            pl.BlockSpec((3, rows, rows), lambda i: (0, 0, 0)),
        ],
        out_specs=[
            pl.BlockSpec((D_MODEL // 2, tile), lambda i: (0, i)),
            pl.BlockSpec((PEER_HEADS, PEER_NKEYS // 2, tile), lambda i: (0, 0, i)),
            pl.BlockSpec((PEER_HEADS, PEER_NKEYS // 2, tile), lambda i: (0, 0, i)),
            pl.BlockSpec((tile // LANES, rows, LANES), lambda i: (i, 0, 0)),
            pl.BlockSpec((tile // LANES, rows, LANES), lambda i: (i, 0, 0)),
        ],
        out_shape=[
            jax.ShapeDtypeStruct((D_MODEL // 2, n), jnp.int32),
            jax.ShapeDtypeStruct((PEER_HEADS, PEER_NKEYS // 2, n), jnp.int32),
            jax.ShapeDtypeStruct((PEER_HEADS, PEER_NKEYS // 2, n), jnp.int32),
            jax.ShapeDtypeStruct((n // LANES, rows, LANES), jnp.int32),
            jax.ShapeDtypeStruct((n // LANES, rows, LANES), jnp.int32),
        ],
        scratch_shapes=[
            pltpu.VMEM((PEER_NKEYS, PEER_HEADS, tile), F32),
            pltpu.VMEM((PEER_NKEYS, PEER_HEADS, tile), F32),
            pltpu.VMEM((PEER_TOPK, PEER_HEADS, tile), F32),
            pltpu.VMEM((PEER_TOPK, PEER_HEADS, tile), F32),
            pltpu.VMEM((npairs, PEER_HEADS, tile), F32),
        ],
        compiler_params=pltpu.CompilerParams(
            dimension_semantics=("arbitrary",),
            vmem_limit_bytes=VMEM_LIMIT_BYTES),
        name="peer_route",
    )(x, gain, wq_t, kbig)


def _dense_kernel(x_ref, xnt_ref, q2h_ref, e2h_ref, thr_ref, e1_ref, u_hbm, vt_hbm,
                  y_ref, acc_ref, w_scr, u_buf, vt_buf, sem, *, layer, tile, echunk):
    j = pl.program_id(1)
    nchunk = pl.num_programs(1)
    step = pl.program_id(0) * nchunk + j
    nstep = pl.num_programs(0) * nchunk

    def table_copies(s):
        c = s % nchunk
        slot = s % TABLE_SLOTS
        return (
            pltpu.make_async_copy(
                u_hbm.at[layer, pl.ds(c * (echunk // 2), echunk // 2), :],
                u_buf.at[slot], sem.at[0, slot]),
            pltpu.make_async_copy(
                vt_hbm.at[layer, :, pl.ds(c * echunk, echunk)],
                vt_buf.at[slot], sem.at[1, slot]))

    @pl.when(step == 0)
    def _():
        for s in range(TABLE_SLOTS - 1):
            @pl.when(s < nstep)
            def _():
                for cp in table_copies(s):
                    cp.start()

    @pl.when(step + (TABLE_SLOTS - 1) < nstep)
    def _():
        for cp in table_copies(step + (TABLE_SLOTS - 1)):
            cp.start()

    for cp in table_copies(step):
        cp.wait()
    u_ref = u_buf.at[step % TABLE_SLOTS]
    vt_ref = vt_buf.at[step % TABLE_SLOTS]

    @pl.when(j == 0)
    def _():
        acc_ref[...] = jnp.zeros_like(acc_ref)

    mxu_w = min(tile, MXU_DIM)
    ngrp = PEER_NKEYS // BF16_ROWS

    def bcast_row(word_ref, r, l0):
        row = word_ref[l0 // LANES, r:r + 1, :]
        return pltpu.bitcast(jnp.broadcast_to(row, (8, LANES)), BF16)

    h_all = jnp.dot(pltpu.bitcast(u_ref[...], BF16), pltpu.bitcast(xnt_ref[...], BF16),
                         preferred_element_type=F32)
    na = echunk // PEER_NKEYS
    ablk = 4
    gblk = 4
    for mb in range(tile // mxu_w):
        m0 = mb * mxu_w
        for lb in range(mxu_w // LANES):
            l0 = m0 + lb * LANES
            for gb in range(ngrp // gblk):
                grp = [gb * gblk + g for g in range(gblk)]
                for ab in range(na // ablk):
                    keys = [ab * ablk + a for a in range(ablk)]
                    gsum = [[jnp.zeros((BF16_ROWS, LANES), BF16) for _ in grp] for _ in keys]
                    for h in range(PEER_HEADS):
                        q2 = [pltpu.bitcast(q2h_ref[h, 8 * g:8 * g + 8, l0:l0 + LANES], BF16)
                              for g in grp]
                        e2 = [pltpu.bitcast(e2h_ref[h, 8 * g:8 * g + 8, l0:l0 + LANES], BF16)
                              for g in grp]
                        for ai, a in enumerate(keys):
                            thr = bcast_row(thr_ref, a * PEER_HEADS + h, l0)
                            e1 = bcast_row(e1_ref, a * PEER_HEADS + h, l0)
                            for g in range(gblk):
                                hit = q2[g] >= thr
                                gsum[ai][g] = gsum[ai][g] + jnp.where(hit, e2[g], 0.0) * e1
                    for ai, a in enumerate(keys):
                        for g in range(gblk):
                            e0 = a * PEER_NKEYS + grp[g] * BF16_ROWS
                            hv = h_all[e0:e0 + BF16_ROWS, l0:l0 + LANES]
                            act = (hv * (1.0 + lax.erf(hv * (2.0 ** -0.5)))).astype(BF16)
                            w_scr[e0:e0 + BF16_ROWS, l0:l0 + LANES] = (
                                jnp.where(gsum[ai][g] > 0.0, act, 0.0) * gsum[ai][g])
    acc_ref[...] += jnp.dot(pltpu.bitcast(vt_ref[...], BF16), w_scr[...],
                            preferred_element_type=F32)

    @pl.when(j == pl.num_programs(1) - 1)
    def _():
        y_ref[...] = x_ref[...] + acc_ref[...].T


def _peer_dense(x, xnt, q2h, e2h, thr, e1, u_words, vt_words, *, layer, tile, echunk):
    n = x.shape[0]
    nt = n // tile
    nchunk = PEER_N_EXPERTS // echunk
    crows = (echunk // PEER_NKEYS) * PEER_HEADS
    kern = functools.partial(_dense_kernel, layer=layer, tile=tile, echunk=echunk)
    return pl.pallas_call(
        kern,
        grid=(nt, nchunk),
        in_specs=[
            pl.BlockSpec((tile, D_MODEL), lambda i, j: (i, 0)),
            pl.BlockSpec((D_MODEL // 2, tile), lambda i, j: (0, i)),
            pl.BlockSpec((PEER_HEADS, PEER_NKEYS // 2, tile), lambda i, j: (0, 0, i)),
            pl.BlockSpec((PEER_HEADS, PEER_NKEYS // 2, tile), lambda i, j: (0, 0, i)),
            pl.BlockSpec((tile // LANES, crows, LANES), lambda i, j: (i, j, 0)),
            pl.BlockSpec((tile // LANES, crows, LANES), lambda i, j: (i, j, 0)),
            pl.BlockSpec(memory_space=pl.ANY),
            pl.BlockSpec(memory_space=pl.ANY),
        ],
        out_specs=pl.BlockSpec((tile, D_MODEL), lambda i, j: (i, 0)),
        out_shape=jax.ShapeDtypeStruct((n, D_MODEL), F32),
        scratch_shapes=[pltpu.VMEM((D_MODEL, tile), F32),
                        pltpu.VMEM((echunk, tile), BF16),
                        pltpu.VMEM((TABLE_SLOTS, echunk // 2, D_MODEL), jnp.int32),
                        pltpu.VMEM((TABLE_SLOTS, D_MODEL // 2, echunk), jnp.int32),
                        pltpu.SemaphoreType.DMA((2, TABLE_SLOTS))],
        compiler_params=pltpu.CompilerParams(
            dimension_semantics=("arbitrary", "arbitrary"),
            vmem_limit_bytes=VMEM_LIMIT_BYTES),
        name="peer_dense",
    )(x, xnt, q2h, e2h, thr, e1, u_words, vt_words)


def _peer(x, gain, wq_t, kbig, u_words, vt_words, *, layer, route_tile, dense_tile, echunk):
    xnt, q2h, e2h, thr, e1 = _peer_route(x, gain, wq_t, kbig, tile=route_tile)
    return _peer_dense(x, xnt, q2h, e2h, thr, e1, u_words, vt_words, layer=layer,
                       tile=dense_tile, echunk=echunk)


def _even_params(norm_g, w_in, conv_w, q_gain, k_gain, sinks, w_out):
    qcols = np.array([1536 + (j + 4 * hf) * HEAD_DIM + d
                      for j in range(4) for hf in range(2) for d in range(HEAD_DIM)])
    cols = np.concatenate([np.arange(1536), qcols, np.arange(2048, EVEN_IN_DIM)])
    orow = np.array([CONV_DIM + (j + 4 * hf) * HEAD_DIM + d
                     for j in range(4) for hf in range(2) for d in range(HEAD_DIM)])
    rows = np.concatenate([np.arange(CONV_DIM), orow])
    blk = lambda n: jnp.asarray(
        (np.arange(n)[:, None] // HEAD_DIM) == (np.arange(n)[None, :] // HEAD_DIM)).astype(BF16)
    return dict(
        gain=norm_g.reshape(1, D_MODEL),
        win=w_in[:, cols].astype(BF16),
        convw=conv_w,
        qg=jnp.tile(q_gain, N_Q_HEADS).reshape(1, ATTN_DIM),
        kg=jnp.tile(k_gain, N_KV_HEADS).reshape(1, KV_DIM),
        sink_rows=jnp.repeat(sinks, CHUNK).reshape(N_Q_HEADS * CHUNK, 1),
        hsum_q=blk(ATTN_DIM),
        hsum_k=blk(KV_DIM),
        wout=w_out[rows, :].astype(BF16),
    )


def _pack_tables_kernel(u_ref, v_ref, uo_ref, vo_ref):
    uo_ref[0] = pltpu.bitcast(u_ref[0].astype(BF16), jnp.int32)
    vo_ref[0] = pltpu.bitcast(v_ref[0].T.astype(BF16), jnp.int32)


def _pack_tables(u_tab, v_tab, *, eblk):
    nl, ne, d = u_tab.shape
    return pl.pallas_call(
        _pack_tables_kernel,
        grid=(nl, ne // eblk),
        in_specs=[pl.BlockSpec((1, eblk, d), lambda l, i: (l, i, 0)),
                  pl.BlockSpec((1, eblk, d), lambda l, i: (l, i, 0))],
        out_specs=[pl.BlockSpec((1, eblk // 2, d), lambda l, i: (l, i, 0)),
                   pl.BlockSpec((1, d // 2, eblk), lambda l, i: (l, 0, i))],
        out_shape=[jax.ShapeDtypeStruct((nl, ne // 2, d), jnp.int32),
                   jax.ShapeDtypeStruct((nl, d // 2, ne), jnp.int32)],
        compiler_params=pltpu.CompilerParams(
            dimension_semantics=("arbitrary", "arbitrary"),
            vmem_limit_bytes=VMEM_LIMIT_BYTES),
        name="pack_tables",
    )(u_tab, v_tab)


def _peer_params(norm_g, w_query, sub_keys):
    wq_t = w_query.T.reshape(PEER_HEADS, 2, 128, D_MODEL).transpose(1, 0, 2, 3)
    wq_t = wq_t.reshape(2 * PEER_HEADS * 128, D_MODEL).astype(BF16)
    eye = jnp.eye(PEER_HEADS, dtype=sub_keys.dtype)
    kbig = jnp.einsum('hpkd,hg->pkhgd', sub_keys, eye).reshape(
        2, PEER_NKEYS * PEER_HEADS, PEER_HEADS * 128)
    khm = jnp.einsum('hkd,hg->hkgd', sub_keys[:, 1], eye).reshape(
        1, PEER_HEADS * PEER_NKEYS, PEER_HEADS * 128)
    kbig = jnp.concatenate([kbig, khm], axis=0).astype(BF16)
    return dict(gain=norm_g.reshape(1, D_MODEL), wq_t=wq_t, kbig=kbig)


def kernel(x_prompt, x_sample, cache_conv, cache_k, cache_v, state_hgrn, norm_mix, norm_ffn,
           even_w_in, even_conv_w, even_q_gain, even_k_gain, even_sinks, even_w_out,
           hgrn_w_in, hgrn_lb, hgrn_out_gain, hgrn_w_out,
           peer_w_query, peer_sub_keys, peer_u, peer_v):
    bp, sp, _ = x_prompt.shape
    bs, ss, _ = x_sample.shape

    ev = _even_params(norm_mix[0], even_w_in[0], even_conv_w[0], even_q_gain[0],
                      even_k_gain[0], even_sinks[0], even_w_out[0])
    u_words, vt_words = _pack_tables(peer_u, peer_v, eblk=PEER_PACK_BLOCK)
    pe = [_peer_params(norm_ffn[l], peer_w_query[l], peer_sub_keys[l]) for l in range(2)]
    lbs = jax.nn.softmax(hgrn_lb.astype(F32), axis=0)
    lbs = jnp.cumsum(lbs, axis=0) - lbs[0]
    hg = dict(gain=norm_mix[1].reshape(1, D_MODEL), win=hgrn_w_in[0].astype(BF16),
              lb=lbs[1].reshape(1, D_MODEL), out_gain=hgrn_out_gain[0].reshape(1, D_MODEL),
              wout=hgrn_w_out[0].astype(BF16))

    def peer(x2d, l, route_tile, dense_tile):
        return _peer(x2d, pe[l]['gain'], pe[l]['wq_t'], pe[l]['kbig'], u_words, vt_words,
                     layer=l, route_tile=route_tile, dense_tile=dense_tile,
                     echunk=PEER_EXPERT_CHUNK)

    zc = jnp.zeros((bp, 2, CONV_DIM), F32)
    zkv = jnp.zeros((bp, WINDOW, KV_DIM), F32)
    rt = MIXER_ROW_TILE
    x, conv_p, k_p, v_p = _even_layer(x_prompt, zc, zkv, zkv, **ev,
                                      tile=rt, valid_rows=rt, has_cache=False)
    x = peer(x.reshape(bp * sp, D_MODEL), 0, PEER_ROUTE_TILE, PEER_DENSE_TILE)
    s0 = jnp.zeros((bp, HGRN_HEADS, HGRN_DK, HGRN_DK), F32)
    x, s_p = _hgrn_layer(x.reshape(bp, sp, D_MODEL), s0, **hg, tile=rt, valid_rows=rt)
    y_prompt = peer(x.reshape(bp * sp, D_MODEL), 1, PEER_ROUTE_TILE, PEER_DENSE_TILE)
    y_prompt = y_prompt.reshape(bp, sp, D_MODEL)

    ns = bs * ss
    xs = jnp.pad(x_sample, ((0, 0), (0, CHUNK - ss), (0, 0)))
    xs, conv_s, k_s, v_s = _even_layer(
        xs, cache_conv[0], cache_k[0].reshape(bs, WINDOW, KV_DIM),
        cache_v[0].reshape(bs, WINDOW, KV_DIM), **ev, tile=CHUNK, valid_rows=ss, has_cache=True)
    xs = peer(xs[:, :ss].reshape(ns, D_MODEL), 0, ns, ns).reshape(bs, ss, D_MODEL)
    xs = jnp.pad(xs, ((0, 0), (0, HGRN_CHUNK - ss), (0, 0)))
    xs, s_s = _hgrn_layer(xs, state_hgrn[0], **hg, tile=HGRN_CHUNK, valid_rows=ss)
    y_sample = peer(xs[:, :ss].reshape(ns, D_MODEL), 1, ns, ns).reshape(bs, ss, D_MODEL)

    kv5 = lambda a, b: a.reshape(1, b, WINDOW, N_KV_HEADS, HEAD_DIM)
    return (y_prompt, y_sample, conv_p[None], kv5(k_p, bp), kv5(v_p, bp), s_p[None],
            conv_s[None], kv5(k_s, bs), kv5(v_s, bs), s_s[None])
```
